```python
import math
import jax, jax.numpy as jnp
from jax import lax
import numpy as np

D_MODEL = 1024
BATCH = 2
SEQ = 8192
DEPTH = 1
DEC_BATCH = 128
DEC_SEQ = 4
PAST_LEN = 16384
PAGE_SIZE = 128

N_META = 16
D_MIX = D_MODEL
M_HEADS = 4
M_DQK = 64
M_DV = 128
M_CHUNK = 64
A_HEADS = 8
A_NOPE = 64
A_ROPE = 32
A_DV = 64
Q_LORA = 256
KV_LORA = 256
ROPE_BASE = 10000.0
Q_BLOCK = 128
ATTN_SCALE = (A_NOPE + A_ROPE) ** -0.5
D_FF = 2816
ALPHA = (2.0 * DEPTH) ** 0.25
BETA = (8.0 * DEPTH) ** -0.25
LN_EPS = 1e-5
RMS_EPS = 1e-6
IN_SIZES = (M_HEADS * M_DQK, M_HEADS * M_DQK, M_HEADS * M_DV, M_HEADS * M_DV, M_HEADS, M_HEADS, Q_LORA, KV_LORA, A_ROPE)
D_IN = 2 * M_HEADS * M_DQK + 2 * M_HEADS * M_DV + 2 * M_HEADS + Q_LORA + KV_LORA + A_ROPE

kernel_name = "hymba_mlstm_mla_macaron_deepnorm_step"


def layer_norm(x, g, b):
    xf = x.astype(jnp.float32)
    mu = jnp.mean(xf, -1, keepdims=True)
    var = jnp.mean(jnp.square(xf - mu), -1, keepdims=True)
    return ((xf - mu) * lax.rsqrt(var + LN_EPS) * g + b).astype(x.dtype)


def rms_norm(x, g):
    xf = x.astype(jnp.float32)
    return (xf * lax.rsqrt(jnp.mean(xf * xf, -1, keepdims=True) + RMS_EPS) * g).astype(x.dtype)


def swiglu(x, w1, w3, w2):
    return (jax.nn.silu(x @ w1) * (x @ w3)) @ w2


def rope(x, pos):
    half = A_ROPE // 2
    inv = ROPE_BASE ** (-jnp.arange(half, dtype=jnp.float32) / half)
    ang = pos.astype(jnp.float32)[:, None] * inv[None, :]
    shape = (ang.shape[0],) + (1,) * (x.ndim - 3) + (half,)
    cos, sin = jnp.cos(ang).reshape(shape), jnp.sin(ang).reshape(shape)
    xf = x.astype(jnp.float32)
    x1, x2 = xf[..., :half], xf[..., half:]
    return jnp.concatenate([x1 * cos - x2 * sin, x1 * sin + x2 * cos], -1).astype(x.dtype)


def mlstm_chunk_step(carry, chunk):
    c_mat, n_vec, m_run = carry
    q, k, v, ig, lf = chunk
    L = q.shape[2]
    cum = jnp.cumsum(lf, axis=-1)
    causal = jnp.tril(jnp.ones((L, L), dtype=bool))
    d = jnp.where(causal, cum[..., :, None] - cum[..., None, :] + ig[..., None, :], -jnp.inf)
    inter = cum + m_run[..., None]
    m_t = jnp.maximum(inter, jnp.max(d, axis=-1))
    s = jnp.einsum("bhtd,bhsd->bhts", q, k) * jnp.exp(d - m_t[..., None])
    w_inter = jnp.exp(inter - m_t)
    num = jnp.einsum("bhts,bhsv->bhtv", s, v) + w_inter[..., None] * jnp.einsum("bhtd,bhdv->bhtv", q, c_mat)
    qn = jnp.sum(s, -1) + w_inter * jnp.einsum("bhtd,bhd->bht", q, n_vec)
    h = num / jnp.maximum(jnp.abs(qn), jnp.exp(-m_t))[..., None]
    last = cum[..., -1]
    g = last[..., None] - cum + ig
    m_new = jnp.maximum(last + m_run, jnp.max(g, -1))
    decay = jnp.exp(last + m_run - m_new)
    wk = k * jnp.exp(g - m_new[..., None])[..., None]
    c_new = decay[..., None, None] * c_mat + jnp.einsum("bhsd,bhsv->bhdv", wk, v)
    n_new = decay[..., None] * n_vec + jnp.sum(wk, axis=2)
    return (c_new, n_new, m_new), h


def mlstm_scan(q, k, v, ig, lf, state, chunk):
    B, T = q.shape[0], q.shape[1]
    L = math.gcd(chunk, T)
    nc = T // L

    def to_chunks(a):
        a = a.reshape((B, nc, L) + a.shape[2:])
        return jnp.moveaxis(jnp.moveaxis(a, 3, 2), 1, 0)

    state, h = lax.scan(mlstm_chunk_step, state, tuple(to_chunks(a) for a in (q, k, v, ig, lf)))
    h = jnp.moveaxis(jnp.moveaxis(h, 0, 1), 2, 3)
    return h.reshape(B, T, M_HEADS, M_DV), state


def mixer_inputs(h, pos, w_in, b_gates, q_norm_g, w_uq, kv_norm_g):
    B, T = h.shape[0], h.shape[1]
    f32 = jnp.float32
    z = h @ w_in
    idx = np.cumsum(IN_SIZES)[:-1].tolist()
    qm, km, vm, om, ig, fg, cq, ckv, kr = jnp.split(z, idx, axis=-1)
    qm = qm.reshape(B, T, M_HEADS, M_DQK).astype(f32)
    km = km.reshape(B, T, M_HEADS, M_DQK).astype(f32) * (M_DQK ** -0.5)
    vm = vm.reshape(B, T, M_HEADS, M_DV).astype(f32)
    ig = ig.astype(f32) + b_gates[:M_HEADS]
    lf = jax.nn.log_sigmoid(fg.astype(f32) + b_gates[M_HEADS:])
    qa = (rms_norm(cq, q_norm_g) @ w_uq).reshape(B, T, A_HEADS, A_NOPE + A_ROPE)
    q_nope = qa[..., :A_NOPE]
    q_rope = rope(qa[..., A_NOPE:], pos)
    c_kv = rms_norm(ckv, kv_norm_g)
    k_rope = rope(kr, pos)
    return qm, km, vm, ig, lf, om, q_nope, q_rope, c_kv, k_rope


def mixer_output(hm, om, attn, mlstm_norm_g, w_out):
    B, T = om.shape[0], om.shape[1]
    hm = rms_norm(hm, mlstm_norm_g.reshape(M_HEADS, M_DV)).reshape(B, T, M_HEADS * M_DV)
    hm = (hm * jax.nn.sigmoid(om.astype(jnp.float32))).astype(om.dtype)
    cat = jnp.concatenate([hm, attn.reshape(B, T, A_HEADS * A_DV).astype(om.dtype)], -1)
    return cat @ w_out


def mla_prompt(q_nope, q_rope, c_kv, k_rope, w_uk, w_uv):
    B, T = c_kv.shape[0], c_kv.shape[1]
    k_nope = jnp.einsum("bkc,chd->bkhd", c_kv, w_uk)
    v = jnp.einsum("bkc,chd->bkhd", c_kv, w_uv)

    def attend(qn, qr, qpos, kn, kr, vv):
        kpos = jnp.arange(kn.shape[1])
        s = (jnp.einsum("bqhd,bkhd->bhqk", qn, kn) + jnp.einsum("bqhr,bkr->bhqk", qr, kr)).astype(jnp.float32) * ATTN_SCALE
        s = jnp.where(qpos[:, None] >= kpos[None, :], s, -jnp.inf)
        p = jax.nn.softmax(s, axis=-1).astype(vv.dtype)
        return jnp.einsum("bhqk,bkhd->bqhd", p, vv)

    out_meta = attend(q_nope[:, :N_META], q_rope[:, :N_META], jnp.arange(N_META),
                      k_nope[:, :N_META], k_rope[:, :N_META], v[:, :N_META])
    n_real = T - N_META
    n_blk = n_real // Q_BLOCK

    def blocks(a):
        return jnp.moveaxis(a[:, N_META:].reshape((B, n_blk, Q_BLOCK) + a.shape[2:]), 1, 0)

    qpos = (N_META + jnp.arange(n_real)).reshape(n_blk, Q_BLOCK)
    out_real = lax.map(lambda a: attend(a[0], a[1], a[2], k_nope, k_rope, v), (blocks(q_nope), blocks(q_rope), qpos))
    out_real = jnp.moveaxis(out_real, 0, 1).reshape(B, n_real, A_HEADS, A_DV)
    return jnp.concatenate([out_meta, out_real], 1)


def mla_sample(q_nope, q_rope, c_new, kr_new, cache_lat, cache_kr, page_table, w_uk, w_uv):
    Bd, T = c_new.shape[0], c_new.shape[1]
    c_past = cache_lat[page_table].reshape(Bd, -1, KV_LORA)
    kr_past = cache_kr[page_table].reshape(Bd, -1, A_ROPE)
    P = c_past.shape[1]
    q_lat = jnp.einsum("bthd,chd->bthc", q_nope, w_uk)
    s_past = jnp.einsum("bthc,bpc->bhtp", q_lat, c_past) + jnp.einsum("bthr,bpr->bhtp", q_rope, kr_past)
    s_new = jnp.einsum("bthc,bsc->bhts", q_lat, c_new) + jnp.einsum("bthr,bsr->bhts", q_rope, kr_new)
    s_new = jnp.where(jnp.tril(jnp.ones((T, T), dtype=bool)), s_new.astype(jnp.float32), -jnp.inf)
    s = jnp.concatenate([s_past.astype(jnp.float32), s_new], -1) * ATTN_SCALE
    p = jax.nn.softmax(s, axis=-1).astype(c_new.dtype)
    o_lat = jnp.einsum("bhtp,bpc->bthc", p[..., :P], c_past) + jnp.einsum("bhts,bsc->bthc", p[..., P:], c_new)
    return jnp.einsum("bthc,chd->bthd", o_lat, w_uv)


def mixer_prompt(h, pos, w_in, b_gates, mlstm_norm_g, q_norm_g, w_uq, kv_norm_g, w_uk, w_uv, w_out):
    qm, km, vm, ig, lf, om, q_nope, q_rope, c_kv, k_rope = mixer_inputs(h, pos, w_in, b_gates, q_norm_g, w_uq, kv_norm_g)
    B = h.shape[0]
    f32 = jnp.float32
    state0 = (jnp.zeros((B, M_HEADS, M_DQK, M_DV), f32), jnp.zeros((B, M_HEADS, M_DQK), f32), jnp.zeros((B, M_HEADS), f32))
    mix = (qm, km, vm, ig, lf)
    h_meta, state = mlstm_scan(*[a[:, :N_META] for a in mix], state0, N_META)
    h_real, state = mlstm_scan(*[a[:, N_META:] for a in mix], state, M_CHUNK)
    hm = jnp.concatenate([h_meta, h_real], 1)
    attn = mla_prompt(q_nope, q_rope, c_kv, k_rope, w_uk, w_uv)
    y = mixer_output(hm, om, attn, mlstm_norm_g, w_out)
    return y, c_kv, k_rope, state


def mixer_sample(h, pos, cache_lat, cache_kr, c0, n0, m0, page_table,
                 w_in, b_gates, mlstm_norm_g, q_norm_g, w_uq, kv_norm_g, w_uk, w_uv, w_out):
    qm, km, vm, ig, lf, om, q_nope, q_rope, c_kv, k_rope = mixer_inputs(h, pos, w_in, b_gates, q_norm_g, w_uq, kv_norm_g)
    f32 = jnp.float32
    state0 = (c0.astype(f32), n0.astype(f32), m0.astype(f32))
    hm, state = mlstm_scan(qm, km, vm, ig, lf, state0, M_CHUNK)
    attn = mla_sample(q_nope, q_rope, c_kv, k_rope, cache_lat, cache_kr, page_table, w_uk, w_uv)
    y = mixer_output(hm, om, attn, mlstm_norm_g, w_out)
    return y, c_kv, k_rope, state


def setup_inputs(seed: int = 0) -> dict:
    key = jax.random.key(seed)
    ks = jax.random.split(key, 32)
    f32 = jnp.float32
    n_pages = PAST_LEN // PAGE_SIZE
    n_used = DEC_BATCH * n_pages
    n_pool = n_used + (n_used + 3) // 4

    def nrm(k, shape, scale=1.0):
        return scale * jax.random.normal(k, shape, f32)

    page_table = jax.random.permutation(ks[7], n_pool)[:n_used].reshape(DEC_BATCH, n_pages).astype(jnp.int32)
    return {
        "x_prompt": nrm(ks[0], (BATCH, SEQ, D_MODEL)),
        "x_sample": nrm(ks[1], (DEC_BATCH, DEC_SEQ, D_MODEL)),
        "cache_kv_latent": nrm(ks[2], (DEPTH, n_pool, PAGE_SIZE, KV_LORA)),
        "cache_k_rope": nrm(ks[3], (DEPTH, n_pool, PAGE_SIZE, A_ROPE)),
        "state_mlstm_C": nrm(ks[4], (DEPTH, DEC_BATCH, M_HEADS, M_DQK, M_DV), 0.5),
        "state_mlstm_n": nrm(ks[5], (DEPTH, DEC_BATCH, M_HEADS, M_DQK), 0.5),
        "state_mlstm_m": nrm(ks[6], (DEPTH, DEC_BATCH, M_HEADS), 0.5),
        "page_table": page_table,
        "meta_tokens": nrm(ks[8], (N_META, D_MODEL)),
        "ffn1_w1": nrm(ks[9], (DEPTH, D_MODEL, D_FF), D_MODEL ** -0.5),
        "ffn1_w3": nrm(ks[10], (DEPTH, D_MODEL, D_FF), D_MODEL ** -0.5),
        "ffn1_w2": nrm(ks[11], (DEPTH, D_FF, D_MODEL), BETA * D_FF ** -0.5),
        "ln1_g": 1.0 + nrm(ks[12], (DEPTH, D_MODEL), 0.01),
        "ln1_b": nrm(ks[13], (DEPTH, D_MODEL), 0.01),
        "w_in": nrm(ks[14], (DEPTH, D_MODEL, D_IN), D_MODEL ** -0.5),
        "b_gates": jnp.concatenate([nrm(ks[15], (DEPTH, M_HEADS), 0.1), 3.0 + nrm(ks[16], (DEPTH, M_HEADS), 0.5)], -1),
        "mlstm_norm_g": 1.0 + nrm(ks[17], (DEPTH, M_HEADS * M_DV), 0.01),
        "q_norm_g": 1.0 + nrm(ks[18], (DEPTH, Q_LORA), 0.01),
        "w_uq": nrm(ks[19], (DEPTH, Q_LORA, A_HEADS * (A_NOPE + A_ROPE)), Q_LORA ** -0.5),
        "kv_norm_g": 1.0 + nrm(ks[20], (DEPTH, KV_LORA), 0.01),
        "w_uk": nrm(ks[21], (DEPTH, KV_LORA, A_HEADS, A_NOPE), KV_LORA ** -0.5),
        "w_uv": nrm(ks[22], (DEPTH, KV_LORA, A_HEADS, A_DV), KV_LORA ** -0.5),
        "w_out": nrm(ks[23], (DEPTH, D_MIX, D_MODEL), BETA * D_MIX ** -0.5),
        "ln2_g": 1.0 + nrm(ks[24], (DEPTH, D_MODEL), 0.01),
        "ln2_b": nrm(ks[25], (DEPTH, D_MODEL), 0.01),
        "ffn2_w1": nrm(ks[26], (DEPTH, D_MODEL, D_FF), D_MODEL ** -0.5),
        "ffn2_w3": nrm(ks[27], (DEPTH, D_MODEL, D_FF), D_MODEL ** -0.5),
        "ffn2_w2": nrm(ks[28], (DEPTH, D_FF, D_MODEL), BETA * D_FF ** -0.5),
        "ln3_g": 1.0 + nrm(ks[29], (DEPTH, D_MODEL), 0.01),
        "ln3_b": nrm(ks[30], (DEPTH, D_MODEL), 0.01),
    }


def reference(x_prompt, x_sample, cache_kv_latent, cache_k_rope, state_mlstm_C, state_mlstm_n, state_mlstm_m,
              page_table, meta_tokens, ffn1_w1, ffn1_w3, ffn1_w2, ln1_g, ln1_b, w_in, b_gates, mlstm_norm_g,
              q_norm_g, w_uq, kv_norm_g, w_uk, w_uv, w_out, ln2_g, ln2_b, ffn2_w1, ffn2_w3, ffn2_w2, ln3_g, ln3_b):
    B = x_prompt.shape[0]
    meta = jnp.broadcast_to(meta_tokens[None].astype(x_prompt.dtype), (B, N_META, D_MODEL))
    xp = jnp.concatenate([meta, x_prompt], 1)
    xs = x_sample
    pos_p = jnp.arange(xp.shape[1])
    pos_s = page_table.shape[1] * PAGE_SIZE + jnp.arange(xs.shape[1])
    kvp, krp, cp, nP, mp_, kvs, krs, cs, ns, ms = ([] for _ in range(10))
    for l in range(DEPTH):
        mix_w = (w_in[l], b_gates[l], mlstm_norm_g[l], q_norm_g[l], w_uq[l], kv_norm_g[l], w_uk[l], w_uv[l], w_out[l])
        xp = layer_norm(ALPHA * xp + 0.5 * swiglu(xp, ffn1_w1[l], ffn1_w3[l], ffn1_w2[l]), ln1_g[l], ln1_b[l])
        xs = layer_norm(ALPHA * xs + 0.5 * swiglu(xs, ffn1_w1[l], ffn1_w3[l], ffn1_w2[l]), ln1_g[l], ln1_b[l])
        yp, c_kv_p, k_r_p, st_p = mixer_prompt(xp, pos_p, *mix_w)
        ys, c_kv_s, k_r_s, st_s = mixer_sample(xs, pos_s, cache_kv_latent[l], cache_k_rope[l], state_mlstm_C[l],
                                                state_mlstm_n[l], state_mlstm_m[l], page_table, *mix_w)
        xp = layer_norm(ALPHA * xp + yp, ln2_g[l], ln2_b[l])
        xs = layer_norm(ALPHA * xs + ys, ln2_g[l], ln2_b[l])
        xp = layer_norm(ALPHA * xp + 0.5 * swiglu(xp, ffn2_w1[l], ffn2_w3[l], ffn2_w2[l]), ln3_g[l], ln3_b[l])
        xs = layer_norm(ALPHA * xs + 0.5 * swiglu(xs, ffn2_w1[l], ffn2_w3[l], ffn2_w2[l]), ln3_g[l], ln3_b[l])
        kvp.append(c_kv_p); krp.append(k_r_p)
        cp.append(st_p[0].astype(state_mlstm_C.dtype)); nP.append(st_p[1].astype(state_mlstm_n.dtype)); mp_.append(st_p[2].astype(state_mlstm_m.dtype))
        kvs.append(c_kv_s); krs.append(k_r_s)
        cs.append(st_s[0].astype(state_mlstm_C.dtype)); ns.append(st_s[1].astype(state_mlstm_n.dtype)); ms.append(st_s[2].astype(state_mlstm_m.dtype))
    y_prompt = xp[:, N_META:]
    return (y_prompt, xs, jnp.stack(kvp), jnp.stack(krp), jnp.stack(cp), jnp.stack(nP), jnp.stack(mp_),
            jnp.stack(kvs), jnp.stack(krs), jnp.stack(cs), jnp.stack(ns), jnp.stack(ms))
```

```python
import functools
import math

import numpy as np
import jax
import jax.numpy as jnp
from jax import lax
from jax.experimental import pallas as pl
from jax.experimental.pallas import tpu as pltpu

F32 = jnp.float32
BF16 = jnp.bfloat16

D_MODEL = 1024
N_META = 16
M_HEADS = 4
M_DQK = 64
M_DV = 128
A_HEADS = 8
A_NOPE = 64
A_ROPE = 32
A_DV = 64
Q_LORA = 256
KV_LORA = 256
ROPE_BASE = 10000.0
PAGE_SIZE = 128
D_FF = 2816
DEPTH = 1
ALPHA = (2.0 * DEPTH) ** 0.25
LN_EPS = 1e-5
RMS_EPS = 1e-6
ATTN_SCALE = (A_NOPE + A_ROPE) ** -0.5
LOG2E = math.log2(math.e)

LANES = 128
HEAD_TILE = LANES
FF_CHUNK = 256
MLSTM_CHUNK = 256
ATTN_BLOCK = 512
PAGES_PER_STEP = 16
VMEM_LIMIT = 56 * 1024 * 1024

_C_Q = 0
_C_V = _C_Q + M_HEADS * HEAD_TILE
_C_O = _C_V + M_HEADS * M_DV
_C_CQ = _C_O + M_HEADS * M_DV
_C_CKV = _C_CQ + Q_LORA
_C_G = _C_CKV + KV_LORA
_C_KRA = _C_G + LANES
_C_KRB = _C_KRA + LANES
_C_END = _C_KRB + LANES


def _cparams(sem):
    return pltpu.CompilerParams(dimension_semantics=sem, vmem_limit_bytes=VMEM_LIMIT)


def _const_spec(shape):
    nd = len(shape)
    return pl.BlockSpec(shape, lambda *_: (0,) * nd, pipeline_mode=pl.Buffered(1))


def _layer_norm(r, g, b):
    mu = jnp.mean(r, axis=-1, keepdims=True)
    d = r - mu
    var = jnp.mean(d * d, axis=-1, keepdims=True)
    return d * lax.rsqrt(var + LN_EPS) * g + b


def _rms_norm(x, g):
    return x * lax.rsqrt(jnp.mean(x * x, axis=-1, keepdims=True) + RMS_EPS) * g


def _dot(a, b):
    return jnp.dot(a, b, preferred_element_type=F32)


def _dot_nt(a, b):
    return lax.dot_general(a, b, (((1,), (1,)), ((), ())), preferred_element_type=F32)


def _swiglu_ln(x, w1_ref, w3_ref, w2_ref, g, b):
    xb = x.astype(BF16)
    y = None
    for j in range(D_FF // FF_CHUNK):
        sl = slice(j * FF_CHUNK, (j + 1) * FF_CHUNK)
        a = _dot(xb, w1_ref[:, sl])
        c = _dot(xb, w3_ref[:, sl])
        hj = (a * jax.nn.sigmoid(a) * c).astype(BF16)
        yj = _dot(hj, w2_ref[sl, :])
        y = yj if y is None else y + yj
    return _layer_norm(ALPHA * x + 0.5 * y, g, b)


def _ffn_kernel(x_ref, w1_ref, w3_ref, w2_ref, g_ref, b_ref, o_ref):
    o_ref[...] = _swiglu_ln(x_ref[...], w1_ref, w3_ref, w2_ref, g_ref[...], b_ref[...])


def _ffn_ln(x, w1, w3, w2, g, b, tm):
    n = x.shape[0]
    row = pl.BlockSpec((tm, D_MODEL), lambda i: (i, 0))
    return pl.pallas_call(
        _ffn_kernel,
        grid=(n // tm,),
        in_specs=[row, _const_spec(w1.shape), _const_spec(w3.shape), _const_spec(w2.shape),
                  _const_spec(g.shape), _const_spec(b.shape)],
        out_specs=row,
        out_shape=jax.ShapeDtypeStruct((n, D_MODEL), F32),
        compiler_params=_cparams(("parallel",)),
        name="ffn_ln",
    )(x, w1, w3, w2, g, b)


def _mixin_kernel(x_ref, w_ref, wt_ref, wgt_ref, bg_ref, bgt_ref, qg_ref, wq_ref, kvg_ref, wkv_ref,
                  cos_ref, sin_ref,
                  q_ref, kt_ref, v_ref, o_ref, g_ref, gt_ref, aq_ref, ak_ref, av_ref, ckv_ref, kr_ref):
    xb = x_ref[...].astype(BF16)
    z = _dot(xb, w_ref[...])
    q_ref[...] = z[:, _C_Q:_C_V].astype(BF16)
    v_ref[...] = z[:, _C_V:_C_O].astype(BF16)
    o_ref[...] = z[:, _C_O:_C_CQ]
    def gates(pre, is_forget):
        logsig = jnp.minimum(pre, 0.0) - jnp.log1p(jnp.exp(-jnp.abs(pre)))
        return jnp.where(is_forget, logsig, pre)
    gc = z[:, _C_G:_C_G + 2 * M_HEADS] + bg_ref[...]
    g_ref[...] = gates(gc, lax.broadcasted_iota(jnp.int32, gc.shape, 1) >= M_HEADS)
    gr = _dot_nt(wgt_ref[...], xb)[:2 * M_HEADS] + bgt_ref[...]
    gt_ref[...] = gates(gr, lax.broadcasted_iota(jnp.int32, gr.shape, 0) >= M_HEADS)
    kt_ref[...] = _dot_nt(wt_ref[...], xb).astype(BF16)
    cos = cos_ref[...]
    sin = sin_ref[...]
    cq = _rms_norm(z[:, _C_CQ:_C_CKV], qg_ref[...]).astype(BF16)
    qa = _dot(cq, wq_ref[...])
    ckv = _rms_norm(z[:, _C_CKV:_C_G], kvg_ref[...])
    ckv_ref[...] = ckv
    kv = _dot(ckv.astype(BF16), wkv_ref[...])
    kr = z[:, _C_KRA:_C_KRB] * cos + z[:, _C_KRB:_C_END] * sin
    kr_ref[...] = kr[:, :A_ROPE]
    lane = lax.broadcasted_iota(jnp.int32, (1, HEAD_TILE), 1)
    nh = A_HEADS * HEAD_TILE
    for h in range(A_HEADS):
        sl = slice(h * HEAD_TILE, (h + 1) * HEAD_TILE)
        sw = slice(nh + h * HEAD_TILE, nh + (h + 1) * HEAD_TILE)
        aq_ref[h] = ((qa[:, sl] * cos + qa[:, sw] * sin) * (ATTN_SCALE * LOG2E)).astype(BF16)
        ak_ref[h] = (kv[:, sl] + kr).astype(BF16)
        one_lane = A_DV if h % 2 == 0 else 0
        av_ref[h] = (kv[:, sw] + (lane == one_lane).astype(F32)).astype(BF16)


def _mixer_inputs(x, wts, cos, sin, tm, n_tab_blocks):
    n = x.shape[0]
    row = lambda w: pl.BlockSpec((tm, w), lambda i: (i, 0))
    colb = lambda r: pl.BlockSpec((r, tm), lambda i: (0, i))
    head = pl.BlockSpec((A_HEADS, tm, HEAD_TILE), lambda i: (0, i, 0))
    tab = pl.BlockSpec((tm, HEAD_TILE), lambda i: (i % n_tab_blocks, 0))
    consts = [wts["w_main"], wts["w_kt"], wts["w_gt"], wts["bg_row"], wts["bg_col"], wts["q_norm_g"],
              wts["w_q2"], wts["kv_norm_g"], wts["w_kv2"]]
    hq = M_HEADS * HEAD_TILE
    hv = M_HEADS * M_DV
    out_shape = (
        jax.ShapeDtypeStruct((n, hq), BF16),
        jax.ShapeDtypeStruct((hq, n), BF16),
        jax.ShapeDtypeStruct((n, hv), BF16),
        jax.ShapeDtypeStruct((n, hv), F32),
        jax.ShapeDtypeStruct((n, 2 * M_HEADS), F32),
        jax.ShapeDtypeStruct((2 * M_HEADS, n), F32),
        jax.ShapeDtypeStruct((A_HEADS, n, HEAD_TILE), BF16),
        jax.ShapeDtypeStruct((A_HEADS, n, HEAD_TILE), BF16),
        jax.ShapeDtypeStruct((A_HEADS, n, HEAD_TILE), BF16),
        jax.ShapeDtypeStruct((n, KV_LORA), F32),
        jax.ShapeDtypeStruct((n, A_ROPE), F32),
    )
    out_specs = (row(hq), colb(hq), row(hv), row(hv), row(2 * M_HEADS), colb(2 * M_HEADS),
                 head, head, head, row(KV_LORA), row(A_ROPE))
    return pl.pallas_call(
        _mixin_kernel,
        grid=(n // tm,),
        in_specs=[row(D_MODEL)] + [_const_spec(c.shape) for c in consts] + [tab, tab],
        out_specs=out_specs,
        out_shape=out_shape,
        compiler_params=_cparams(("parallel",)),
        name="mixer_inputs",
    )(x, *consts, cos, sin)


def _split3(x):
    hi = x.astype(BF16)
    r1 = x - hi.astype(F32)
    mid = r1.astype(BF16)
    lo = (r1 - mid.astype(F32)).astype(BF16)
    return hi, mid, lo


def _mlstm_kernel(q_ref, kt_ref, v_ref, g_ref, gt_ref, c0_ref, m0_ref, h_ref, c_ref, m_ref, c_sc, m_sc,
                  *, chunk):
    ci = pl.program_id(1)

    @pl.when(ci == 0)
    def _():
        c_sc[...] = c0_ref[0]
        m_sc[...] = m0_ref[0]

    L = chunk
    row = lax.broadcasted_iota(jnp.int32, (L, L), 0)
    col = lax.broadcasted_iota(jnp.int32, (L, L), 1)
    causal = row >= col
    tril = causal.astype(BF16)
    triu = (row <= col).astype(BF16)
    g = g_ref[...]
    gt = gt_ref[...]
    cum_c = sum(_dot(tril, p) for p in _split3(g))
    cum_r = sum(_dot(p, triu) for p in _split3(gt))
    ones_col = (lax.broadcasted_iota(jnp.int32, (L, LANES), 1) == 0).astype(BF16)
    for h in range(M_HEADS):
        sl = slice(h * HEAD_TILE, (h + 1) * HEAD_TILE)
        qh = q_ref[:, sl]
        kth = kt_ref[sl, :]
        v_aug = jnp.concatenate([v_ref[:, h * M_DV:(h + 1) * M_DV], ones_col], axis=1)
        ig_r = gt[h:h + 1, :]
        lf_r = gt[M_HEADS + h:M_HEADS + h + 1, :]
        cr = cum_r[M_HEADS + h:M_HEADS + h + 1, :]
        cc = cum_c[:, M_HEADS + h:M_HEADS + h + 1]
        m_run = m_sc[h, 0:1, 0:1]
        d = jnp.where(causal, cc - cr + ig_r, -jnp.inf)
        inter = cc + m_run
        m_t = jnp.maximum(inter, jnp.max(d, axis=1, keepdims=True))
        s = (_dot(qh, kth) * jnp.exp(d - m_t)).astype(BF16)
        w_inter = jnp.exp(inter - m_t)
        c_state = c_sc[h]
        r = _dot(s, v_aug) + w_inter * _dot(qh, c_state.astype(BF16))
        qn = r[:, M_DV:M_DV + 1]
        den = jnp.maximum(jnp.abs(qn), jnp.exp(-m_t))
        h_ref[:, h * M_DV:(h + 1) * M_DV] = r[:, :M_DV] / den
        last = jnp.sum(lf_r, axis=1, keepdims=True)
        g_r = last - cr + ig_r
        m_new = jnp.maximum(last + m_run, jnp.max(g_r, axis=1, keepdims=True))
        decay = jnp.exp(last + m_run - m_new)
        wkt = (kth.astype(F32) * jnp.exp(g_r - m_new)).astype(BF16)
        c_sc[h] = decay * c_state + _dot(wkt, v_aug)
        m_sc[h] = jnp.broadcast_to(m_new, (8, LANES))

    @pl.when(ci == pl.num_programs(1) - 1)
    def _():
        c_ref[0] = c_sc[...]
        m_ref[0] = m_sc[...]


def _mlstm_scan(q, kt, v, g, gt, c0, m0, batch, seq, chunk):
    nc = seq // chunk
    hq = M_HEADS * HEAD_TILE
    hv = M_HEADS * M_DV
    row = lambda w: pl.BlockSpec((chunk, w), lambda b, c: (b * nc + c, 0))
    colb = lambda r: pl.BlockSpec((r, chunk), lambda b, c: (0, b * nc + c))
    st_in = lambda s: pl.BlockSpec((1,) + s, lambda b, c: (0,) * (len(s) + 1))
    st_out = lambda s: pl.BlockSpec((1,) + s, lambda b, c: (b,) + (0,) * len(s))
    cs, ms = (M_HEADS, HEAD_TILE, 2 * LANES), (M_HEADS, 8, LANES)
    return pl.pallas_call(
        functools.partial(_mlstm_kernel, chunk=chunk),
        grid=(batch, nc),
        in_specs=[row(hq), colb(hq), row(hv), row(2 * M_HEADS), colb(2 * M_HEADS), st_in(cs), st_in(ms)],
        out_specs=(row(hv), st_out(cs), st_out(ms)),
        out_shape=(jax.ShapeDtypeStruct((batch * seq, hv), F32),
                   jax.ShapeDtypeStruct((batch,) + cs, F32),
                   jax.ShapeDtypeStruct((batch,) + ms, F32)),
        scratch_shapes=[pltpu.VMEM(cs, F32), pltpu.VMEM(ms, F32)],
        compiler_params=_cparams(("parallel", "arbitrary")),
        name="mlstm_scan",
    )(q, kt, v, g, gt, c0, m0)


def _mlstm_step_kernel(q_ref, k_ref, kt_ref, v_ref, gc_ref, gr_ref, c0_ref, n0_ref, m0_ref,
                       h_ref, c_ref, n_ref, m_ref):
    T = q_ref.shape[1]
    row = lax.broadcasted_iota(jnp.int32, (T, T), 0)
    col = lax.broadcasted_iota(jnp.int32, (T, T), 1)
    causal = (row >= col)[None]
    q = q_ref[...]
    k = k_ref[...]
    v = v_ref[...]
    ig_c, lf_c = gc_ref[:, :, 0:1], gc_ref[:, :, 1:2]
    ig_r, lf_r = gr_ref[:, 0:1, :], gr_ref[:, 1:2, :]
    m0 = m0_ref[...]
    c0 = c0_ref[...]
    n0 = n0_ref[...]
    cum_c = jnp.sum(jnp.where(causal, lf_r, 0.0), axis=2, keepdims=True)
    cum_r = jnp.sum(jnp.where((row <= col)[None], lf_c, 0.0), axis=1, keepdims=True)
    d = jnp.where(causal, cum_c - cum_r + ig_r, -jnp.inf)
    inter = cum_c + m0
    m_t = jnp.maximum(inter, jnp.max(d, axis=2, keepdims=True))
    qk = jnp.einsum("gtd,gsd->gts", q, k, preferred_element_type=F32)
    s = qk * jnp.exp(d - m_t)
    w_inter = jnp.exp(inter - m_t)
    num = (jnp.einsum("gts,gsv->gtv", s.astype(BF16), v, preferred_element_type=F32)
           + w_inter * jnp.einsum("gtd,gdv->gtv", q, c0.astype(BF16), preferred_element_type=F32))
    qn = jnp.sum(s, axis=2, keepdims=True) + w_inter * jnp.sum(q.astype(F32) * n0, axis=2, keepdims=True)
    h_ref[...] = num / jnp.maximum(jnp.abs(qn), jnp.exp(-m_t))
    last = cum_c[:, T - 1:T, :]
    g_c = last - cum_c + ig_c
    g_r = last - cum_r + ig_r
    m_new = jnp.maximum(last + m0, jnp.max(g_c, axis=1, keepdims=True))
    decay = jnp.exp(last + m0 - m_new)
    wk = k.astype(F32) * jnp.exp(g_c - m_new)
    wkt = (kt_ref[...].astype(F32) * jnp.exp(g_r - m_new)).astype(BF16)
    c_ref[...] = decay * c0 + jnp.einsum("gds,gsv->gdv", wkt, v, preferred_element_type=F32)
    n_ref[...] = decay * n0 + jnp.sum(wk, axis=1, keepdims=True)
    m_ref[...] = m_new


def _mlstm_step(q, k, kt, v, gc, gr, c0, n0, m0, gb):
    gtot, T = q.shape[0], q.shape[1]
    blk = lambda a, b: pl.BlockSpec((gb, a, b), lambda i: (i, 0, 0))
    return pl.pallas_call(
        _mlstm_step_kernel,
        grid=(gtot // gb,),
        in_specs=[blk(T, M_DQK), blk(T, M_DQK), blk(M_DQK, T), blk(T, M_DV), blk(T, 2), blk(2, T),
                  blk(M_DQK, M_DV), blk(1, M_DQK), blk(1, 1)],
        out_specs=(blk(T, M_DV), blk(M_DQK, M_DV), blk(1, M_DQK), blk(1, 1)),
        out_shape=(jax.ShapeDtypeStruct((gtot, T, M_DV), F32),
                   jax.ShapeDtypeStruct((gtot, M_DQK, M_DV), F32),
                   jax.ShapeDtypeStruct((gtot, 1, M_DQK), F32),
                   jax.ShapeDtypeStruct((gtot, 1, 1), F32)),
        compiler_params=_cparams(("parallel",)),
        name="mlstm_step",
    )(q, k, kt, v, gc, gr, c0, n0, m0)


def _attn_kernel(qi_tab, ki_tab, q_ref, k_ref, v_ref, km_ref, vm_ref, o_ref, acc_sc, m_sc):
    step = pl.program_id(1)
    qi = qi_tab[step]
    ki = ki_tab[step]
    tq = q_ref.shape[1]

    @pl.when(ki == 0)
    def _():
        for h in range(A_HEADS):
            s = _dot_nt(q_ref[h], km_ref[h])
            m = jnp.max(s, axis=1, keepdims=True)
            acc_sc[h] = _dot(jnp.exp2(s - m).astype(BF16), vm_ref[h])
            m_sc[h] = m

    def update(masked):
        if masked:
            row = lax.broadcasted_iota(jnp.int32, (tq, tq), 0)
            col = lax.broadcasted_iota(jnp.int32, (tq, tq), 1)
            keep = row >= col
        for h in range(A_HEADS):
            s = _dot_nt(q_ref[h], k_ref[h])
            if masked:
                s = jnp.where(keep, s, -jnp.inf)
            m_prev = m_sc[h]
            m_new = jnp.maximum(m_prev, jnp.max(s, axis=1, keepdims=True))
            p = jnp.exp2(s - m_new).astype(BF16)
            acc_sc[h] = jnp.exp2(m_prev - m_new) * acc_sc[h] + _dot(p, v_ref[h])
            m_sc[h] = m_new

    @pl.when(ki < qi)
    def _():
        update(False)

    @pl.when(ki == qi)
    def _():
        update(True)
        lane = lax.broadcasted_iota(jnp.int32, (1, HEAD_TILE), 1)
        for j in range(A_HEADS // 2):
            ae = acc_sc[2 * j]
            ao = acc_sc[2 * j + 1]
            oe = ae / ae[:, A_DV:A_DV + 1]
            oo = ao / ao[:, 0:1]
            o_ref[:, j * HEAD_TILE:(j + 1) * HEAD_TILE] = jnp.where(lane < A_DV, oe, oo).astype(o_ref.dtype)


def _attention_prompt(aq, ak, av, k_meta, v_meta, batch, seq):
    blk = ATTN_BLOCK
    nq = seq // blk
    qi_np, ki_np = zip(*[(i, j) for i in range(nq) for j in range(i + 1)])
    qi_tab = jnp.asarray(np.array(qi_np, np.int32))
    ki_tab = jnp.asarray(np.array(ki_np, np.int32))
    head = lambda tab: pl.BlockSpec((A_HEADS, blk, HEAD_TILE),
                                    lambda b, s, qt, kt: (0, b * nq + (qt, kt)[tab][s], 0))
    meta = pl.BlockSpec((A_HEADS, N_META, HEAD_TILE), lambda b, s, qt, kt: (0, 0, 0))
    grid_spec = pltpu.PrefetchScalarGridSpec(
        num_scalar_prefetch=2,
        grid=(batch, len(qi_np)),
        in_specs=[head(0), head(1), head(1), meta, meta],
        out_specs=pl.BlockSpec((blk, A_HEADS * A_DV), lambda b, s, qt, kt: (b * nq + qt[s], 0)),
        scratch_shapes=[pltpu.VMEM((A_HEADS, blk, HEAD_TILE), F32), pltpu.VMEM((A_HEADS, blk, 1), F32)],
    )
    return pl.pallas_call(
        _attn_kernel,
        grid_spec=grid_spec,
        out_shape=jax.ShapeDtypeStruct((batch * seq, A_HEADS * A_DV), BF16),
        compiler_params=_cparams(("parallel", "arbitrary")),
        name="attn_prompt",
    )(qi_tab, ki_tab, aq, ak, av, k_meta, v_meta)


def _absorb_kernel(q_ref, wukt_ref, o_ref):
    for h in range(A_HEADS):
        o_ref[h] = _dot(q_ref[h], wukt_ref[h]).astype(BF16)


def _absorb_q(aq, wukt):
    n = aq.shape[1]
    return pl.pallas_call(
        _absorb_kernel,
        out_shape=jax.ShapeDtypeStruct((A_HEADS, n, KV_LORA), BF16),
        compiler_params=pltpu.CompilerParams(vmem_limit_bytes=VMEM_LIMIT),
        name="absorb_q",
    )(aq, wukt)


def _decode_attn_kernel(pt_ref, ql_ref, qr_ref, cn_ref, kn_ref, *rest, n_pages):
    lat_refs = rest[:n_pages]
    kr_refs = rest[n_pages:2 * n_pages]
    o_ref, acc_sc, m_sc, l_sc = rest[2 * n_pages:]
    j = pl.program_id(1)
    ql = ql_ref[0]
    qr = qr_ref[0]

    @pl.when(j == 0)
    def _():
        acc_sc[...] = jnp.zeros_like(acc_sc)
        m_sc[...] = jnp.full_like(m_sc, -jnp.inf)
        l_sc[...] = jnp.zeros_like(l_sc)

    lats = [r[0].astype(BF16) for r in lat_refs]
    s = jnp.concatenate(
        [_dot_nt(ql, lat) + _dot_nt(qr, kr[0].astype(BF16)) for lat, kr in zip(lats, kr_refs)], axis=1)
    m_prev = m_sc[...]
    m_new = jnp.maximum(m_prev, jnp.max(s, axis=1, keepdims=True))
    p = jnp.exp2(s - m_new)
    alpha = jnp.exp2(m_prev - m_new)
    l_sc[...] = alpha * l_sc[...] + jnp.sum(p, axis=1, keepdims=True)
    pb = p.astype(BF16)
    pv = sum(_dot(pb[:, i * PAGE_SIZE:(i + 1) * PAGE_SIZE], lat) for i, lat in enumerate(lats))
    acc_sc[...] = alpha * acc_sc[...] + pv
    m_sc[...] = m_new

    @pl.when(j == pl.num_programs(1) - 1)
    def _():
        T = cn_ref.shape[1]
        cn = cn_ref[0].astype(BF16).astype(F32)
        kn = kn_ref[0].astype(BF16).astype(F32)
        qlf = ql.astype(F32)
        qrf = qr.astype(F32)
        t_of_row = lax.broadcasted_iota(jnp.int32, (ql.shape[0], 1), 0) % T
        s_new = []
        for t in range(T):
            st = (jnp.sum(qlf * cn[t:t + 1], axis=1, keepdims=True)
                  + jnp.sum(qrf * kn[t:t + 1], axis=1, keepdims=True))
            s_new.append(jnp.where(t_of_row >= t, st, -jnp.inf))
        m_prev = m_sc[...]
        m_fin = m_prev
        for st in s_new:
            m_fin = jnp.maximum(m_fin, st)
        alpha = jnp.exp2(m_prev - m_fin)
        acc = alpha * acc_sc[...]
        l = alpha * l_sc[...]
        for t, st in enumerate(s_new):
            pt = jnp.exp2(st - m_fin)
            l = l + pt
            acc = acc + pt * cn[t:t + 1]
        o_ref[0] = acc / l


def _decode_attention(page_table, q_lat, q_rope, c_new, kr_new, cache_lat, cache_kr):
    nb, n_pages_total = page_table.shape
    P = PAGES_PER_STEP
    rows = q_lat.shape[1]
    per_b = lambda a, b: pl.BlockSpec((1, a, b), lambda i, j, pt: (i, 0, 0))
    page = lambda w, k: pl.BlockSpec((1, PAGE_SIZE, w), lambda i, j, pt, k=k: (pt[i, j * P + k], 0, 0))
    grid_spec = pltpu.PrefetchScalarGridSpec(
        num_scalar_prefetch=1,
        grid=(nb, n_pages_total // P),
        in_specs=[per_b(rows, KV_LORA), per_b(rows, A_ROPE), per_b(c_new.shape[1], KV_LORA),
                  per_b(kr_new.shape[1], A_ROPE)]
                 + [page(KV_LORA, k) for k in range(P)] + [page(A_ROPE, k) for k in range(P)],
        out_specs=per_b(rows, KV_LORA),
        scratch_shapes=[pltpu.VMEM((rows, KV_LORA), F32), pltpu.VMEM((rows, 1), F32),
                        pltpu.VMEM((rows, 1), F32)],
    )
    return pl.pallas_call(
        functools.partial(_decode_attn_kernel, n_pages=P),
        grid_spec=grid_spec,
        out_shape=jax.ShapeDtypeStruct((nb, rows, KV_LORA), F32),
        compiler_params=_cparams(("parallel", "arbitrary")),
        name="attn_decode",
    )(page_table, q_lat, q_rope, c_new, kr_new, *([cache_lat] * P), *([cache_kr] * P))


def _lat2attn_kernel(o_ref, wuv_ref, a_ref):
    for j in range(A_HEADS // 2):
        pair = (_dot(o_ref[2 * j].astype(BF16), wuv_ref[2 * j])
                + _dot(o_ref[2 * j + 1].astype(BF16), wuv_ref[2 * j + 1]))
        a_ref[:, j * HEAD_TILE:(j + 1) * HEAD_TILE] = pair.astype(a_ref.dtype)


def _lat2attn(o_lat, wuv_tiles):
    n = o_lat.shape[1]
    return pl.pallas_call(
        _lat2attn_kernel,
        out_shape=jax.ShapeDtypeStruct((n, A_HEADS * A_DV), BF16),
        compiler_params=pltpu.CompilerParams(vmem_limit_bytes=VMEM_LIMIT),
        name="lat2attn",
    )(o_lat, wuv_tiles)


def _mixout_kernel(x_ref, hm_ref, om_ref, at_ref, ng_ref, wo_ref, g2_ref, b2_ref,
                   w1_ref, w3_ref, w2_ref, g3_ref, b3_ref, o_ref):
    x = x_ref[...]
    parts = []
    for h in range(M_HEADS):
        sl = slice(h * M_DV, (h + 1) * M_DV)
        hn = _rms_norm(hm_ref[:, sl], ng_ref[:, sl])
        parts.append((hn * jax.nn.sigmoid(om_ref[:, sl])).astype(BF16))
    cat = jnp.concatenate(parts + [at_ref[...]], axis=1)
    x2 = _layer_norm(ALPHA * x + _dot(cat, wo_ref[...]), g2_ref[...], b2_ref[...])
    o_ref[...] = _swiglu_ln(x2, w1_ref, w3_ref, w2_ref, g3_ref[...], b3_ref[...])


def _mixer_output_ffn(x, hm, om, attn, wts, tm):
    n = x.shape[0]
    row = lambda w: pl.BlockSpec((tm, w), lambda i: (i, 0))
    consts = [wts["mlstm_norm_g"], wts["w_out"], wts["ln2_g"], wts["ln2_b"],
              wts["ffn2_w1"], wts["ffn2_w3"], wts["ffn2_w2"], wts["ln3_g"], wts["ln3_b"]]
    hv = M_HEADS * M_DV
    return pl.pallas_call(
        _mixout_kernel,
        grid=(n // tm,),
        in_specs=[row(D_MODEL), row(hv), row(hv), row(A_HEADS * A_DV)] + [_const_spec(c.shape) for c in consts],
        out_specs=row(D_MODEL),
        out_shape=jax.ShapeDtypeStruct((n, D_MODEL), F32),
        compiler_params=_cparams(("parallel",)),
        name="mixer_output_ffn",
    )(x, hm, om, attn, *consts)


def _pack_weights(w_in, b_gates, mlstm_norm_g, q_norm_g, w_uq, kv_norm_g, w_uk, w_uv, w_out):
    z = lambda r, c: jnp.zeros((r, c), F32)
    o_q, o_k, o_v, o_o = 0, M_HEADS * M_DQK, 2 * M_HEADS * M_DQK, 2 * M_HEADS * M_DQK + M_HEADS * M_DV
    o_g = o_o + M_HEADS * M_DV
    o_cq = o_g + 2 * M_HEADS
    o_ckv = o_cq + Q_LORA
    o_kr = o_ckv + KV_LORA
    half = A_ROPE // 2
    pad_q = HEAD_TILE - M_DQK
    q_cols = [jnp.concatenate([w_in[:, o_q + h * M_DQK:o_q + (h + 1) * M_DQK], z(D_MODEL, pad_q)], 1)
              for h in range(M_HEADS)]
    kr = w_in[:, o_kr:o_kr + A_ROPE]
    kr_sw = jnp.concatenate([kr[:, half:], kr[:, :half]], 1)
    w_main = jnp.concatenate(
        q_cols + [w_in[:, o_v:o_g], w_in[:, o_cq:o_kr],
                  w_in[:, o_g:o_cq], z(D_MODEL, LANES - 2 * M_HEADS),
                  kr, z(D_MODEL, LANES - A_ROPE), kr_sw, z(D_MODEL, LANES - A_ROPE)], 1)
    assert w_main.shape[1] == _C_END
    k_rows = [jnp.concatenate([w_in[:, o_k + h * M_DQK:o_k + (h + 1) * M_DQK].T * (M_DQK ** -0.5),
                               z(pad_q, D_MODEL)], 0) for h in range(M_HEADS)]
    w_kt = jnp.concatenate(k_rows, 0)
    w_gt = jnp.concatenate([w_in[:, o_g:o_cq].T, z(16 - 2 * M_HEADS, D_MODEL)], 0)
    dq = A_NOPE + A_ROPE
    q_tiles, q_sw_tiles, uk_tiles, uv_tiles, ukt_tiles, uv2_tiles = [], [], [], [], [], []
    zq = lambda c: z(Q_LORA, c)
    for h in range(A_HEADS):
        nope = w_uq[:, h * dq:h * dq + A_NOPE]
        x1 = w_uq[:, h * dq + A_NOPE:h * dq + A_NOPE + half]
        x2 = w_uq[:, h * dq + A_NOPE + half:(h + 1) * dq]
        q_tiles.append(jnp.concatenate([x1, x2, nope, zq(HEAD_TILE - dq)], 1))
        q_sw_tiles.append(jnp.concatenate([x2, x1, zq(HEAD_TILE - A_ROPE)], 1))
        uk_h = w_uk[:, h, :]
        uk_tiles.append(jnp.concatenate([zq(A_ROPE), uk_h, zq(HEAD_TILE - dq)], 1))
        uv_h = w_uv[:, h, :]
        uv_tile = (jnp.concatenate([uv_h, zq(HEAD_TILE - A_DV)], 1) if h % 2 == 0
                   else jnp.concatenate([zq(HEAD_TILE - A_DV), uv_h], 1))
        uv_tiles.append(uv_tile)
        ukt_tiles.append(jnp.concatenate([z(A_ROPE, KV_LORA), uk_h.T, z(HEAD_TILE - dq, KV_LORA)], 0))
        uv2_tiles.append(uv_tile)
    bf = lambda a: a.astype(BF16)
    return {
        "w_main": bf(w_main), "w_kt": bf(w_kt), "w_gt": bf(w_gt),
        "bg_row": b_gates.reshape(1, -1), "bg_col": b_gates.reshape(-1, 1),
        "q_norm_g": q_norm_g.reshape(1, -1), "kv_norm_g": kv_norm_g.reshape(1, -1),
        "w_q2": bf(jnp.concatenate(q_tiles + q_sw_tiles, 1)),
        "w_kv2": bf(jnp.concatenate(uk_tiles + uv_tiles, 1)),
        "w_ukt": bf(jnp.stack(ukt_tiles)),
        "w_uv_tiles": bf(jnp.stack(uv2_tiles)),
        "mlstm_norm_g": mlstm_norm_g.reshape(1, -1), "w_out": bf(w_out),
    }


def _rope_tables(pos):
    half = A_ROPE // 2
    inv = ROPE_BASE ** (-jnp.arange(half, dtype=F32) / half)
    ang = pos.astype(F32)[:, None] * inv[None, :]
    cos, sin = jnp.cos(ang), jnp.sin(ang)
    n = pos.shape[0]
    cos_t = jnp.concatenate([cos, cos, jnp.ones((n, A_NOPE), F32), jnp.zeros((n, HEAD_TILE - A_NOPE - A_ROPE), F32)], 1)
    sin_t = jnp.concatenate([-sin, sin, jnp.zeros((n, HEAD_TILE - A_ROPE), F32)], 1)
    return cos_t, sin_t


def kernel(x_prompt, x_sample, cache_kv_latent, cache_k_rope, state_mlstm_C, state_mlstm_n, state_mlstm_m, page_table, meta_tokens, ffn1_w1, ffn1_w3, ffn1_w2, ln1_g, ln1_b, w_in, b_gates, mlstm_norm_g, q_norm_g, w_uq, kv_norm_g, w_uk, w_uv, w_out, ln2_g, ln2_b, ffn2_w1, ffn2_w3, ffn2_w2, ln3_g, ln3_b):
    assert w_in.shape[0] == DEPTH == 1
    B, S, _ = x_prompt.shape
    Bd, Td, _ = x_sample.shape
    n_real, n_dec = B * S, Bd * Td
    l = 0
    vec = lambda a: a[l].reshape(1, -1)
    wts = _pack_weights(w_in[l], b_gates[l], mlstm_norm_g[l], q_norm_g[l], w_uq[l], kv_norm_g[l],
                        w_uk[l], w_uv[l], w_out[l])
    wts.update({"ln2_g": vec(ln2_g), "ln2_b": vec(ln2_b), "ln3_g": vec(ln3_g), "ln3_b": vec(ln3_b),
                "ffn2_w1": ffn2_w1[l].astype(BF16), "ffn2_w3": ffn2_w3[l].astype(BF16),
                "ffn2_w2": ffn2_w2[l].astype(BF16)})
    f1 = (ffn1_w1[l].astype(BF16), ffn1_w3[l].astype(BF16), ffn1_w2[l].astype(BF16), vec(ln1_g), vec(ln1_b))

    x_real = x_prompt.reshape(n_real, D_MODEL)
    x_extra = jnp.concatenate([meta_tokens.astype(F32), x_sample.reshape(n_dec, D_MODEL)], 0)
    n_extra = x_extra.shape[0]
    tm = 512

    h_real = _ffn_ln(x_real, *f1, tm)
    h_extra = _ffn_ln(x_extra, *f1, n_extra)

    cos_r, sin_r = _rope_tables(N_META + jnp.arange(S))
    pos_e = jnp.concatenate([jnp.arange(N_META),
                             jnp.tile(page_table.shape[1] * PAGE_SIZE + jnp.arange(Td), Bd)])
    cos_e, sin_e = _rope_tables(pos_e)
    (q_r, kt_r, v_r, om_r, g_r, gt_r, aq_r, ak_r, av_r, ckv_r, kr_r) = _mixer_inputs(
        h_real, wts, cos_r, sin_r, tm, S // tm)
    (q_e, kt_e, v_e, om_e, g_e, gt_e, aq_e, ak_e, av_e, ckv_e, kr_e) = _mixer_inputs(
        h_extra, wts, cos_e, sin_e, n_extra, 1)

    cs, ms = (1, M_HEADS, HEAD_TILE, 2 * LANES), (1, M_HEADS, 8, LANES)
    _, c_meta, m_meta = _mlstm_scan(q_e[:N_META], kt_e[:, :N_META], v_e[:N_META], g_e[:N_META],
                                    gt_e[:, :N_META], jnp.zeros(cs, F32), jnp.zeros(ms, F32), 1, N_META, N_META)
    hm_r, c_p, m_p = _mlstm_scan(q_r, kt_r, v_r, g_r, gt_r, c_meta, m_meta, B, S, MLSTM_CHUNK)
    mlstm_c_p = c_p[:, :, :M_DQK, :M_DV][None]
    mlstm_n_p = c_p[:, :, :M_DQK, M_DV][None]
    mlstm_m_p = m_p[:, :, 0, 0][None]

    G = Bd * M_HEADS
    def dec_heads(a, w, used):
        return a.reshape(Bd, Td, M_HEADS, w)[..., :used].transpose(0, 2, 1, 3).reshape(G, Td, used)
    q_d = dec_heads(q_e[N_META:], HEAD_TILE, M_DQK)
    v_d = dec_heads(v_e[N_META:], M_DV, M_DV)
    kt_d = (kt_e[:, N_META:].reshape(M_HEADS, HEAD_TILE, Bd, Td)[:, :M_DQK]
            .transpose(2, 0, 1, 3).reshape(G, M_DQK, Td))
    k_d = kt_d.transpose(0, 2, 1)
    gdec = g_e[N_META:].reshape(Bd, Td, 2, M_HEADS).transpose(0, 3, 1, 2).reshape(G, Td, 2)
    hm_d, c_d, n_d, m_d = _mlstm_step(
        q_d, k_d, kt_d, v_d, gdec, gdec.transpose(0, 2, 1),
        state_mlstm_C[l].reshape(G, M_DQK, M_DV), state_mlstm_n[l].reshape(G, 1, M_DQK),
        state_mlstm_m[l].reshape(G, 1, 1), 64)
    hm_d = hm_d.reshape(Bd, M_HEADS, Td, M_DV).transpose(0, 2, 1, 3).reshape(n_dec, M_HEADS * M_DV)

    attn_r = _attention_prompt(aq_r, ak_r, av_r, ak_e[:, :N_META], av_e[:, :N_META], B, S)

    q_lat = _absorb_q(aq_e[:, N_META:], wts["w_ukt"])
    rows = A_HEADS * Td
    by_seq = lambda a: a.reshape(A_HEADS, Bd, Td, -1).transpose(1, 0, 2, 3).reshape(Bd, rows, a.shape[-1])
    o_lat = _decode_attention(
        page_table, by_seq(q_lat), by_seq(aq_e[:, N_META:, :A_ROPE]),
        ckv_e[N_META:].reshape(Bd, Td, KV_LORA), kr_e[N_META:].reshape(Bd, Td, A_ROPE),
        cache_kv_latent[l], cache_k_rope[l])
    o_lat = o_lat.reshape(Bd, A_HEADS, Td, KV_LORA).transpose(1, 0, 2, 3).reshape(A_HEADS, n_dec, KV_LORA)
    attn_d = _lat2attn(o_lat, wts["w_uv_tiles"])

    y_real = _mixer_output_ffn(h_real, hm_r, om_r, attn_r, wts, tm)
    y_dec = _mixer_output_ffn(h_extra[N_META:], hm_d, om_e[N_META:], attn_d, wts, n_dec)

    bc = lambda a: jnp.broadcast_to(a[None], (B,) + a.shape)
    kv_p = jnp.concatenate([bc(ckv_e[:N_META]), ckv_r.reshape(B, S, KV_LORA)], 1)[None]
    kr_p = jnp.concatenate([bc(kr_e[:N_META]), kr_r.reshape(B, S, A_ROPE)], 1)[None]
    return (y_real.reshape(B, S, D_MODEL), y_dec.reshape(Bd, Td, D_MODEL), kv_p, kr_p,
            mlstm_c_p, mlstm_n_p, mlstm_m_p,
            ckv_e[N_META:].reshape(1, Bd, Td, KV_LORA), kr_e[N_META:].reshape(1, Bd, Td, A_ROPE),
            c_d.reshape(1, Bd, M_HEADS, M_DQK, M_DV), n_d.reshape(1, Bd, M_HEADS, M_DQK),
            m_d.reshape(1, Bd, M_HEADS))
```

```python
import functools
import math

import numpy as np
import jax
import jax.numpy as jnp
from jax import lax
from jax.experimental import pallas as pl
from jax.experimental.pallas import tpu as pltpu

F32 = jnp.float32
BF16 = jnp.bfloat16

D_MODEL = 1024
N_META = 16
M_HEADS = 4
M_DQK = 64
M_DV = 128
A_HEADS = 8
A_NOPE = 64
A_ROPE = 32
A_DV = 64
Q_LORA = 256
KV_LORA = 256
ROPE_BASE = 10000.0
PAGE_SIZE = 128
D_FF = 2816
DEPTH = 1
ALPHA = (2.0 * DEPTH) ** 0.25
LN_EPS = 1e-5
RMS_EPS = 1e-6
ATTN_SCALE = (A_NOPE + A_ROPE) ** -0.5
LOG2E = math.log2(math.e)

LANES = 128
HEAD_TILE = LANES
FF_CHUNK = 256
MLSTM_CHUNK = 256
ATTN_BLOCK = 512
DEC_SUB = 8
DEC_CHUNK = 32
DEC_SLOTS = 4
VMEM_LIMIT = 56 * 1024 * 1024

_C_Q = 0
_C_V = _C_Q + M_HEADS * HEAD_TILE
_C_O = _C_V + M_HEADS * M_DV
_C_CQ = _C_O + M_HEADS * M_DV
_C_CKV = _C_CQ + Q_LORA
_C_G = _C_CKV + KV_LORA
_C_KRA = _C_G + LANES
_C_KRB = _C_KRA + LANES
_C_END = _C_KRB + LANES


def _cparams(sem):
    return pltpu.CompilerParams(dimension_semantics=sem, vmem_limit_bytes=VMEM_LIMIT)


def _const_spec(shape):
    nd = len(shape)
    return pl.BlockSpec(shape, lambda *_: (0,) * nd, pipeline_mode=pl.Buffered(1))


def _layer_norm(r, g, b):
    mu = jnp.mean(r, axis=-1, keepdims=True)
    d = r - mu
    var = jnp.mean(d * d, axis=-1, keepdims=True)
    return d * lax.rsqrt(var + LN_EPS) * g + b


def _rms_norm(x, g):
    return x * lax.rsqrt(jnp.mean(x * x, axis=-1, keepdims=True) + RMS_EPS) * g


def _dot(a, b):
    return jnp.dot(a, b, preferred_element_type=F32)


def _dot_nt(a, b):
    return lax.dot_general(a, b, (((1,), (1,)), ((), ())), preferred_element_type=F32)


def _swiglu_ln(x, w1_ref, w3_ref, w2_ref, g, b):
    xb = x.astype(BF16)
    y = None
    for j in range(D_FF // FF_CHUNK):
        sl = slice(j * FF_CHUNK, (j + 1) * FF_CHUNK)
        a = _dot(xb, w1_ref[:, sl])
        c = _dot(xb, w3_ref[:, sl])
        hj = (a * jax.nn.sigmoid(a) * c).astype(BF16)
        yj = _dot(hj, w2_ref[sl, :])
        y = yj if y is None else y + yj
    return _layer_norm(ALPHA * x + 0.5 * y, g, b)


def _ffn_kernel(x_ref, w1_ref, w3_ref, w2_ref, g_ref, b_ref, o_ref):
    o_ref[...] = _swiglu_ln(x_ref[...], w1_ref, w3_ref, w2_ref, g_ref[...], b_ref[...])


def _ffn_ln(x, w1, w3, w2, g, b, tm):
    n = x.shape[0]
    row = pl.BlockSpec((tm, D_MODEL), lambda i: (i, 0))
    return pl.pallas_call(
        _ffn_kernel,
        grid=(n // tm,),
        in_specs=[row, _const_spec(w1.shape), _const_spec(w3.shape), _const_spec(w2.shape),
                  _const_spec(g.shape), _const_spec(b.shape)],
        out_specs=row,
        out_shape=jax.ShapeDtypeStruct((n, D_MODEL), F32),
        compiler_params=_cparams(("parallel",)),
        name="ffn_ln",
    )(x, w1, w3, w2, g, b)


def _mixin_kernel(x_ref, w_ref, wt_ref, wgt_ref, bg_ref, bgt_ref, qg_ref, wqt_ref, kvg_ref, wuk_ref, wuvt_ref,
                  cos_ref, sin_ref, cost_ref, sint_ref,
                  q_ref, kt_ref, v_ref, o_ref, g_ref, gt_ref, aqt_ref, ak_ref, avt_ref, ckv_ref, kr_ref):
    xb = x_ref[...].astype(BF16)
    z = _dot(xb, w_ref[...])
    q_ref[...] = z[:, _C_Q:_C_V].astype(BF16)
    v_ref[...] = z[:, _C_V:_C_O].astype(BF16)
    o_ref[...] = z[:, _C_O:_C_CQ]
    def gates(pre, is_forget):
        logsig = jnp.minimum(pre, 0.0) - jnp.log1p(jnp.exp(-jnp.abs(pre)))
        return jnp.where(is_forget, logsig, pre)
    gc = z[:, _C_G:_C_G + 2 * M_HEADS] + bg_ref[...]
    g_ref[...] = gates(gc, lax.broadcasted_iota(jnp.int32, gc.shape, 1) >= M_HEADS)
    gr = _dot_nt(wgt_ref[...], xb)[:2 * M_HEADS] + bgt_ref[...]
    gt_ref[...] = gates(gr, lax.broadcasted_iota(jnp.int32, gr.shape, 0) >= M_HEADS)
    kt_ref[...] = _dot_nt(wt_ref[...], xb).astype(BF16)
    cq = _rms_norm(z[:, _C_CQ:_C_CKV], qg_ref[...]).astype(BF16)
    qat = _dot_nt(wqt_ref[...], cq)
    ckv = _rms_norm(z[:, _C_CKV:_C_G], kvg_ref[...])
    ckv_ref[...] = ckv
    ckv_b = ckv.astype(BF16)
    kn = _dot(ckv_b, wuk_ref[...])
    vt = _dot_nt(wuvt_ref[...], ckv_b)
    kr = z[:, _C_KRA:_C_KRB] * cos_ref[...] + z[:, _C_KRB:_C_END] * sin_ref[...]
    kr_ref[...] = kr[:, :A_ROPE]
    cost = cost_ref[...]
    sint = sint_ref[...]
    sub = lax.broadcasted_iota(jnp.int32, (HEAD_TILE, 1), 0)
    nh = A_HEADS * HEAD_TILE
    for h in range(A_HEADS):
        sl = slice(h * HEAD_TILE, (h + 1) * HEAD_TILE)
        sw = slice(nh + h * HEAD_TILE, nh + (h + 1) * HEAD_TILE)
        aqt_ref[h] = ((qat[sl] * cost + qat[sw] * sint) * (ATTN_SCALE * LOG2E)).astype(BF16)
        ak_ref[h] = (kn[:, sl] + kr).astype(BF16)
        one_row = A_DV if h % 2 == 0 else 0
        avt_ref[h] = (vt[sl] + (sub == one_row).astype(F32)).astype(BF16)


def _mixer_inputs(x, wts, tables, tm, n_tab_blocks):
    n = x.shape[0]
    cos, sin = tables
    row = lambda w: pl.BlockSpec((tm, w), lambda i: (i, 0))
    colb = lambda r: pl.BlockSpec((r, tm), lambda i: (0, i))
    head = pl.BlockSpec((A_HEADS, tm, HEAD_TILE), lambda i: (0, i, 0))
    head_t = pl.BlockSpec((A_HEADS, HEAD_TILE, tm), lambda i: (0, 0, i))
    tab = pl.BlockSpec((tm, HEAD_TILE), lambda i: (i % n_tab_blocks, 0))
    tab_t = pl.BlockSpec((HEAD_TILE, tm), lambda i: (0, i % n_tab_blocks))
    consts = [wts["w_main"], wts["w_kt"], wts["w_gt"], wts["bg_row"], wts["bg_col"], wts["q_norm_g"],
              wts["w_q2t"], wts["kv_norm_g"], wts["w_uk_tiles"], wts["w_uvt"]]
    hq = M_HEADS * HEAD_TILE
    hv = M_HEADS * M_DV
    out_shape = (
        jax.ShapeDtypeStruct((n, hq), BF16),
        jax.ShapeDtypeStruct((hq, n), BF16),
        jax.ShapeDtypeStruct((n, hv), BF16),
        jax.ShapeDtypeStruct((n, hv), F32),
        jax.ShapeDtypeStruct((n, 2 * M_HEADS), F32),
        jax.ShapeDtypeStruct((2 * M_HEADS, n), F32),
        jax.ShapeDtypeStruct((A_HEADS, HEAD_TILE, n), BF16),
        jax.ShapeDtypeStruct((A_HEADS, n, HEAD_TILE), BF16),
        jax.ShapeDtypeStruct((A_HEADS, HEAD_TILE, n), BF16),
        jax.ShapeDtypeStruct((n, KV_LORA), F32),
        jax.ShapeDtypeStruct((n, A_ROPE), F32),
    )
    out_specs = (row(hq), colb(hq), row(hv), row(hv), row(2 * M_HEADS), colb(2 * M_HEADS),
                 head_t, head, head_t, row(KV_LORA), row(A_ROPE))
    return pl.pallas_call(
        _mixin_kernel,
        grid=(n // tm,),
        in_specs=[row(D_MODEL)] + [_const_spec(c.shape) for c in consts] + [tab, tab, tab_t, tab_t],
        out_specs=out_specs,
        out_shape=out_shape,
        compiler_params=_cparams(("parallel",)),
        name="mixer_inputs",
    )(x, *consts, cos, sin, cos.T, sin.T)


def _split3(x):
    hi = x.astype(BF16)
    r1 = x - hi.astype(F32)
    mid = r1.astype(BF16)
    lo = (r1 - mid.astype(F32)).astype(BF16)
    return hi, mid, lo


def _mlstm_kernel(q_ref, kt_ref, v_ref, g_ref, gt_ref, c0_ref, m0_ref, h_ref, c_ref, m_ref, c_sc, m_sc,
                  *, chunk):
    ci = pl.program_id(1)

    @pl.when(ci == 0)
    def _():
        c_sc[...] = c0_ref[0]
        m_sc[...] = m0_ref[0]

    L = chunk
    row = lax.broadcasted_iota(jnp.int32, (L, L), 0)
    col = lax.broadcasted_iota(jnp.int32, (L, L), 1)
    causal = row >= col
    tril = causal.astype(BF16)
    triu = (row <= col).astype(BF16)
    g = g_ref[...]
    gt = gt_ref[...]
    cum_c = sum(_dot(tril, p) for p in _split3(g))
    cum_r = sum(_dot(p, triu) for p in _split3(gt))
    ones_col = (lax.broadcasted_iota(jnp.int32, (L, LANES), 1) == 0).astype(BF16)
    for h in range(M_HEADS):
        sl = slice(h * HEAD_TILE, (h + 1) * HEAD_TILE)
        qh = q_ref[:, sl]
        kth = kt_ref[sl, :]
        v_aug = jnp.concatenate([v_ref[:, h * M_DV:(h + 1) * M_DV], ones_col], axis=1)
        ig_r = gt[h:h + 1, :]
        lf_r = gt[M_HEADS + h:M_HEADS + h + 1, :]
        cr = cum_r[M_HEADS + h:M_HEADS + h + 1, :]
        cc = cum_c[:, M_HEADS + h:M_HEADS + h + 1]
        m_run = m_sc[h, 0:1, 0:1]
        d = jnp.where(causal, cc - cr + ig_r, -jnp.inf)
        inter = cc + m_run
        m_t = jnp.maximum(inter, jnp.max(d, axis=1, keepdims=True))
        s = (_dot(qh, kth) * jnp.exp(d - m_t)).astype(BF16)
        w_inter = jnp.exp(inter - m_t)
        c_state = c_sc[h]
        r = _dot(s, v_aug) + w_inter * _dot(qh, c_state.astype(BF16))
        qn = r[:, M_DV:M_DV + 1]
        den = jnp.maximum(jnp.abs(qn), jnp.exp(-m_t))
        h_ref[:, h * M_DV:(h + 1) * M_DV] = r[:, :M_DV] / den
        last = jnp.sum(lf_r, axis=1, keepdims=True)
        g_r = last - cr + ig_r
        m_new = jnp.maximum(last + m_run, jnp.max(g_r, axis=1, keepdims=True))
        decay = jnp.exp(last + m_run - m_new)
        wkt = (kth.astype(F32) * jnp.exp(g_r - m_new)).astype(BF16)
        c_sc[h] = decay * c_state + _dot(wkt, v_aug)
        m_sc[h] = jnp.broadcast_to(m_new, (8, LANES))

    @pl.when(ci == pl.num_programs(1) - 1)
    def _():
        c_ref[0] = c_sc[...]
        m_ref[0] = m_sc[...]


def _mlstm_scan(q, kt, v, g, gt, c0, m0, batch, seq, chunk):
    nc = seq // chunk
    hq = M_HEADS * HEAD_TILE
    hv = M_HEADS * M_DV
    row = lambda w: pl.BlockSpec((chunk, w), lambda b, c: (b * nc + c, 0))
    colb = lambda r: pl.BlockSpec((r, chunk), lambda b, c: (0, b * nc + c))
    st_in = lambda s: pl.BlockSpec((1,) + s, lambda b, c: (0,) * (len(s) + 1))
    st_out = lambda s: pl.BlockSpec((1,) + s, lambda b, c: (b,) + (0,) * len(s))
    cs, ms = (M_HEADS, HEAD_TILE, 2 * LANES), (M_HEADS, 8, LANES)
    return pl.pallas_call(
        functools.partial(_mlstm_kernel, chunk=chunk),
        grid=(batch, nc),
        in_specs=[row(hq), colb(hq), row(hv), row(2 * M_HEADS), colb(2 * M_HEADS), st_in(cs), st_in(ms)],
        out_specs=(row(hv), st_out(cs), st_out(ms)),
        out_shape=(jax.ShapeDtypeStruct((batch * seq, hv), F32),
                   jax.ShapeDtypeStruct((batch,) + cs, F32),
                   jax.ShapeDtypeStruct((batch,) + ms, F32)),
        scratch_shapes=[pltpu.VMEM(cs, F32), pltpu.VMEM(ms, F32)],
        compiler_params=_cparams(("parallel", "arbitrary")),
        name="mlstm_scan",
    )(q, kt, v, g, gt, c0, m0)


def _mlstm_step_kernel(q_ref, k_ref, kt_ref, v_ref, gc_ref, gr_ref, c0_ref, n0_ref, m0_ref,
                       h_ref, c_ref, n_ref, m_ref):
    T = q_ref.shape[1]
    row = lax.broadcasted_iota(jnp.int32, (T, T), 0)
    col = lax.broadcasted_iota(jnp.int32, (T, T), 1)
    causal = (row >= col)[None]
    q = q_ref[...]
    k = k_ref[...]
    v = v_ref[...]
    ig_c, lf_c = gc_ref[:, :, 0:1], gc_ref[:, :, 1:2]
    ig_r, lf_r = gr_ref[:, 0:1, :], gr_ref[:, 1:2, :]
    m0 = m0_ref[...]
    c0 = c0_ref[...]
    n0 = n0_ref[...]
    cum_c = jnp.sum(jnp.where(causal, lf_r, 0.0), axis=2, keepdims=True)
    cum_r = jnp.sum(jnp.where((row <= col)[None], lf_c, 0.0), axis=1, keepdims=True)
    d = jnp.where(causal, cum_c - cum_r + ig_r, -jnp.inf)
    inter = cum_c + m0
    m_t = jnp.maximum(inter, jnp.max(d, axis=2, keepdims=True))
    qk = jnp.einsum("gtd,gsd->gts", q, k, preferred_element_type=F32)
    s = qk * jnp.exp(d - m_t)
    w_inter = jnp.exp(inter - m_t)
    num = (jnp.einsum("gts,gsv->gtv", s.astype(BF16), v, preferred_element_type=F32)
           + w_inter * jnp.einsum("gtd,gdv->gtv", q, c0.astype(BF16), preferred_element_type=F32))
    qn = jnp.sum(s, axis=2, keepdims=True) + w_inter * jnp.sum(q.astype(F32) * n0, axis=2, keepdims=True)
    h_ref[...] = num / jnp.maximum(jnp.abs(qn), jnp.exp(-m_t))
    last = cum_c[:, T - 1:T, :]
    g_c = last - cum_c + ig_c
    g_r = last - cum_r + ig_r
    m_new = jnp.maximum(last + m0, jnp.max(g_c, axis=1, keepdims=True))
    decay = jnp.exp(last + m0 - m_new)
    wk = k.astype(F32) * jnp.exp(g_c - m_new)
    wkt = (kt_ref[...].astype(F32) * jnp.exp(g_r - m_new)).astype(BF16)
    c_ref[...] = decay * c0 + jnp.einsum("gds,gsv->gdv", wkt, v, preferred_element_type=F32)
    n_ref[...] = decay * n0 + jnp.sum(wk, axis=1, keepdims=True)
    m_ref[...] = m_new


def _mlstm_step(q, k, kt, v, gc, gr, c0, n0, m0, gb):
    gtot, T = q.shape[0], q.shape[1]
    blk = lambda a, b: pl.BlockSpec((gb, a, b), lambda i: (i, 0, 0))
    return pl.pallas_call(
        _mlstm_step_kernel,
        grid=(gtot // gb,),
        in_specs=[blk(T, M_DQK), blk(T, M_DQK), blk(M_DQK, T), blk(T, M_DV), blk(T, 2), blk(2, T),
                  blk(M_DQK, M_DV), blk(1, M_DQK), blk(1, 1)],
        out_specs=(blk(T, M_DV), blk(M_DQK, M_DV), blk(1, M_DQK), blk(1, 1)),
        out_shape=(jax.ShapeDtypeStruct((gtot, T, M_DV), F32),
                   jax.ShapeDtypeStruct((gtot, M_DQK, M_DV), F32),
                   jax.ShapeDtypeStruct((gtot, 1, M_DQK), F32),
                   jax.ShapeDtypeStruct((gtot, 1, 1), F32)),
        compiler_params=_cparams(("parallel",)),
        name="mlstm_step",
    )(q, k, kt, v, gc, gr, c0, n0, m0)


def _attn_kernel(qi_tab, ki_tab, qt_ref, k_ref, vt_ref, km_ref, vmt_ref, o_ref, acc_sc, m_sc):
    step = pl.program_id(1)
    qi = qi_tab[step]
    ki = ki_tab[step]
    tq = qt_ref.shape[2]

    @pl.when(ki == 0)
    def _():
        ss = [_dot(km_ref[h], qt_ref[h]) for h in range(A_HEADS)]
        ms = [jnp.max(s, axis=0, keepdims=True) for s in ss]
        for h in range(A_HEADS):
            acc_sc[h] = _dot(vmt_ref[h], jnp.exp2(ss[h] - ms[h]).astype(BF16))
            m_sc[h] = ms[h]

    def update(masked):
        if masked:
            row = lax.broadcasted_iota(jnp.int32, (tq, tq), 0)
            col = lax.broadcasted_iota(jnp.int32, (tq, tq), 1)
            keep = row <= col
        s_next = _dot(k_ref[0], qt_ref[0])
        for h in range(A_HEADS):
            s = s_next
            if h + 1 < A_HEADS:
                s_next = _dot(k_ref[h + 1], qt_ref[h + 1])
            if masked:
                s = jnp.where(keep, s, -jnp.inf)
            m_prev = m_sc[h]
            m_new = jnp.maximum(m_prev, jnp.max(s, axis=0, keepdims=True))
            p = jnp.exp2(s - m_new).astype(BF16)
            acc_sc[h] = jnp.exp2(m_prev - m_new) * acc_sc[h] + _dot(vt_ref[h], p)
            m_sc[h] = m_new

    @pl.when(ki < qi)
    def _():
        update(False)

    @pl.when(ki == qi)
    def _():
        update(True)
        sub = lax.broadcasted_iota(jnp.int32, (HEAD_TILE, 1), 0)
        for j in range(A_HEADS // 2):
            ae = acc_sc[2 * j]
            ao = acc_sc[2 * j + 1]
            oe = ae / ae[A_DV:A_DV + 1, :]
            oo = ao / ao[0:1, :]
            pair = jnp.where(sub < A_DV, oe, oo)
            o_ref[:, j * HEAD_TILE:(j + 1) * HEAD_TILE] = pair.T.astype(o_ref.dtype)


def _attention_prompt(aqt, ak, avt, k_meta, vt_meta, batch, seq):
    blk = ATTN_BLOCK
    nq = seq // blk
    qi_np, ki_np = zip(*[(i, j) for i in range(nq) for j in range(i + 1)])
    qi_tab = jnp.asarray(np.array(qi_np, np.int32))
    ki_tab = jnp.asarray(np.array(ki_np, np.int32))
    rows = lambda tab: pl.BlockSpec((A_HEADS, blk, HEAD_TILE),
                                    lambda b, s, qt, kt: (0, b * nq + (qt, kt)[tab][s], 0))
    cols = lambda tab: pl.BlockSpec((A_HEADS, HEAD_TILE, blk),
                                    lambda b, s, qt, kt: (0, 0, b * nq + (qt, kt)[tab][s]))
    grid_spec = pltpu.PrefetchScalarGridSpec(
        num_scalar_prefetch=2,
        grid=(batch, len(qi_np)),
        in_specs=[cols(0), rows(1), cols(1),
                  pl.BlockSpec((A_HEADS, N_META, HEAD_TILE), lambda b, s, qt, kt: (0, 0, 0)),
                  pl.BlockSpec((A_HEADS, HEAD_TILE, N_META), lambda b, s, qt, kt: (0, 0, 0))],
        out_specs=pl.BlockSpec((blk, A_HEADS * A_DV), lambda b, s, qt, kt: (b * nq + qt[s], 0)),
        scratch_shapes=[pltpu.VMEM((A_HEADS, HEAD_TILE, blk), F32), pltpu.VMEM((A_HEADS, 1, blk), F32)],
    )
    return pl.pallas_call(
        _attn_kernel,
        grid_spec=grid_spec,
        out_shape=jax.ShapeDtypeStruct((batch * seq, A_HEADS * A_DV), BF16),
        compiler_params=_cparams(("parallel", "arbitrary")),
        name="attn_prompt",
    )(qi_tab, ki_tab, aqt, ak, avt, k_meta, vt_meta)


def _absorb_kernel(qt_ref, wuk_ref, o_ref):
    for h in range(A_HEADS):
        o_ref[h] = _dot(wuk_ref[h], qt_ref[h]).astype(BF16)


def _absorb_q(aqt, wuk_tiles):
    n = aqt.shape[2]
    return pl.pallas_call(
        _absorb_kernel,
        out_shape=jax.ShapeDtypeStruct((A_HEADS, KV_LORA, n), BF16),
        compiler_params=pltpu.CompilerParams(vmem_limit_bytes=VMEM_LIMIT),
        name="absorb_q",
    )(aqt, wuk_tiles)


def _decode_attn_kernel(pt_ref, ql_ref, qr_ref, cn_ref, kn_ref, lat_hbm, krt_hbm, o_ref,
                        lat_buf, krt_buf, sem_lat, sem_krt):
    b = pl.program_id(0)
    nb = pl.num_programs(0)
    sub, chunk, n_slots = DEC_SUB, DEC_CHUNK, DEC_SLOTS
    n_sub = pt_ref.shape[1] // sub
    spc = chunk // sub
    n_chunks = n_sub // spc
    ahead = 2 * spc
    assert n_chunks % n_slots == 0 and n_slots >= 4

    def copies(bb, j):
        slot = (j // spc) % n_slots
        for k in range((j % spc) * sub, (j % spc + 1) * sub):
            page = pt_ref[bb, (j // spc) * chunk + k]
            yield pltpu.make_async_copy(lat_hbm.at[page], lat_buf.at[slot, k], sem_lat.at[slot])
            yield pltpu.make_async_copy(krt_hbm.at[page], krt_buf.at[slot, k], sem_krt.at[slot])

    def start(bb, j):
        for cp in copies(bb, j):
            cp.start()

    def wait_chunk(c):
        for j in range(c * spc, (c + 1) * spc):
            for cp in copies(b, j):
                cp.wait()

    @pl.when(b == 0)
    def _():
        for j in range(ahead):
            start(0, j)

    ql = ql_ref[0]
    qr = qr_ref[0]
    rows = ql.shape[0]
    m_run = jnp.full((rows, 1), -jnp.inf, F32)
    l_run = jnp.zeros((rows, 1), F32)
    acc = jnp.zeros((rows, KV_LORA), F32)

    def scores(j):
        slot = (j // spc) % n_slots
        ks = range((j % spc) * sub, (j % spc + 1) * sub)
        lats = [lat_buf[slot, k].astype(BF16) for k in ks]
        s = jnp.concatenate(
            [_dot_nt(ql, lat) + _dot(qr, krt_buf[slot, k].astype(BF16)) for k, lat in zip(ks, lats)],
            axis=1)
        return lats, s

    wait_chunk(0)
    nxt = scores(0)
    for i in range(n_sub):
        lats, s = nxt
        if i + 1 < n_sub:
            if (i + 1) % spc == 0:
                wait_chunk((i + 1) // spc)
            nxt = scores(i + 1)
        j = i + ahead
        if j < n_sub:
            start(b, j)
        else:
            @pl.when(b + 1 < nb)
            def _():
                start(b + 1, j - n_sub)
        m_new = jnp.maximum(m_run, jnp.max(s, axis=1, keepdims=True))
        p = jnp.exp2(s - m_new)
        alpha = jnp.exp2(m_run - m_new)
        l_run = alpha * l_run + jnp.sum(p, axis=1, keepdims=True)
        pb = p.astype(BF16)
        pv = sum(_dot(pb[:, k * PAGE_SIZE:(k + 1) * PAGE_SIZE], lat) for k, lat in enumerate(lats))
        acc = alpha * acc + pv
        m_run = m_new

    T = cn_ref.shape[1]
    cn = cn_ref[0].astype(BF16).astype(F32)
    kn = kn_ref[0].astype(BF16).astype(F32)
    qlf = ql.astype(F32)
    qrf = qr.astype(F32)
    t_of_row = lax.broadcasted_iota(jnp.int32, (rows, 1), 0) % T
    s_new = []
    for t in range(T):
        st = (jnp.sum(qlf * cn[t:t + 1], axis=1, keepdims=True)
              + jnp.sum(qrf * kn[t:t + 1], axis=1, keepdims=True))
        s_new.append(jnp.where(t_of_row >= t, st, -jnp.inf))
    m_fin = m_run
    for st in s_new:
        m_fin = jnp.maximum(m_fin, st)
    alpha = jnp.exp2(m_run - m_fin)
    acc = alpha * acc
    l_run = alpha * l_run
    for t, st in enumerate(s_new):
        pt = jnp.exp2(st - m_fin)
        l_run = l_run + pt
        acc = acc + pt * cn[t:t + 1]
    o_ref[0] = acc / l_run


def _decode_attention(page_table, q_lat, q_rope, c_new, kr_new, cache_lat, cache_krt):
    nb = page_table.shape[0]
    rows = q_lat.shape[1]
    per_b = lambda a, b: pl.BlockSpec((1, a, b), lambda i, pt: (i, 0, 0))
    hbm = pl.BlockSpec(memory_space=pl.ANY)
    grid_spec = pltpu.PrefetchScalarGridSpec(
        num_scalar_prefetch=1,
        grid=(nb,),
        in_specs=[per_b(rows, KV_LORA), per_b(rows, A_ROPE), per_b(c_new.shape[1], KV_LORA),
                  per_b(kr_new.shape[1], A_ROPE), hbm, hbm],
        out_specs=per_b(rows, KV_LORA),
        scratch_shapes=[pltpu.VMEM((DEC_SLOTS, DEC_CHUNK, PAGE_SIZE, KV_LORA), F32),
                        pltpu.VMEM((DEC_SLOTS, DEC_CHUNK, A_ROPE, PAGE_SIZE), F32),
                        pltpu.SemaphoreType.DMA((DEC_SLOTS,)), pltpu.SemaphoreType.DMA((DEC_SLOTS,))],
    )
    return pl.pallas_call(
        _decode_attn_kernel,
        grid_spec=grid_spec,
        out_shape=jax.ShapeDtypeStruct((nb, rows, KV_LORA), F32),
        compiler_params=_cparams(("arbitrary",)),
        name="attn_decode",
    )(page_table, q_lat, q_rope, c_new, kr_new, cache_lat, cache_krt)


def _lat2attn_kernel(o_ref, wuv_ref, a_ref):
    for j in range(A_HEADS // 2):
        pair = (_dot(o_ref[2 * j].astype(BF16), wuv_ref[2 * j])
                + _dot(o_ref[2 * j + 1].astype(BF16), wuv_ref[2 * j + 1]))
        a_ref[:, j * HEAD_TILE:(j + 1) * HEAD_TILE] = pair.astype(a_ref.dtype)


def _lat2attn(o_lat, wuv_tiles):
    n = o_lat.shape[1]
    return pl.pallas_call(
        _lat2attn_kernel,
        out_shape=jax.ShapeDtypeStruct((n, A_HEADS * A_DV), BF16),
        compiler_params=pltpu.CompilerParams(vmem_limit_bytes=VMEM_LIMIT),
        name="lat2attn",
    )(o_lat, wuv_tiles)


def _mixout_kernel(x_ref, hm_ref, om_ref, at_ref, ng_ref, wo_ref, g2_ref, b2_ref,
                   w1_ref, w3_ref, w2_ref, g3_ref, b3_ref, o_ref):
    x = x_ref[...]
    parts = []
    for h in range(M_HEADS):
        sl = slice(h * M_DV, (h + 1) * M_DV)
        hn = _rms_norm(hm_ref[:, sl], ng_ref[:, sl])
        parts.append((hn * jax.nn.sigmoid(om_ref[:, sl])).astype(BF16))
    cat = jnp.concatenate(parts + [at_ref[...]], axis=1)
    x2 = _layer_norm(ALPHA * x + _dot(cat, wo_ref[...]), g2_ref[...], b2_ref[...])
    o_ref[...] = _swiglu_ln(x2, w1_ref, w3_ref, w2_ref, g3_ref[...], b3_ref[...])


def _mixer_output_ffn(x, hm, om, attn, wts, tm):
    n = x.shape[0]
    row = lambda w: pl.BlockSpec((tm, w), lambda i: (i, 0))
    consts = [wts["mlstm_norm_g"], wts["w_out"], wts["ln2_g"], wts["ln2_b"],
              wts["ffn2_w1"], wts["ffn2_w3"], wts["ffn2_w2"], wts["ln3_g"], wts["ln3_b"]]
    hv = M_HEADS * M_DV
    return pl.pallas_call(
        _mixout_kernel,
        grid=(n // tm,),
        in_specs=[row(D_MODEL), row(hv), row(hv), row(A_HEADS * A_DV)] + [_const_spec(c.shape) for c in consts],
        out_specs=row(D_MODEL),
        out_shape=jax.ShapeDtypeStruct((n, D_MODEL), F32),
        compiler_params=_cparams(("parallel",)),
        name="mixer_output_ffn",
    )(x, hm, om, attn, *consts)


def _pack_weights(w_in, b_gates, mlstm_norm_g, q_norm_g, w_uq, kv_norm_g, w_uk, w_uv, w_out):
    z = lambda r, c: jnp.zeros((r, c), F32)
    o_q, o_k, o_v, o_o = 0, M_HEADS * M_DQK, 2 * M_HEADS * M_DQK, 2 * M_HEADS * M_DQK + M_HEADS * M_DV
    o_g = o_o + M_HEADS * M_DV
    o_cq = o_g + 2 * M_HEADS
    o_ckv = o_cq + Q_LORA
    o_kr = o_ckv + KV_LORA
    half = A_ROPE // 2
    pad_q = HEAD_TILE - M_DQK
    q_cols = [jnp.concatenate([w_in[:, o_q + h * M_DQK:o_q + (h + 1) * M_DQK], z(D_MODEL, pad_q)], 1)
              for h in range(M_HEADS)]
    kr = w_in[:, o_kr:o_kr + A_ROPE]
    kr_sw = jnp.concatenate([kr[:, half:], kr[:, :half]], 1)
    w_main = jnp.concatenate(
        q_cols + [w_in[:, o_v:o_g], w_in[:, o_cq:o_kr],
                  w_in[:, o_g:o_cq], z(D_MODEL, LANES - 2 * M_HEADS),
                  kr, z(D_MODEL, LANES - A_ROPE), kr_sw, z(D_MODEL, LANES - A_ROPE)], 1)
    assert w_main.shape[1] == _C_END
    k_rows = [jnp.concatenate([w_in[:, o_k + h * M_DQK:o_k + (h + 1) * M_DQK].T * (M_DQK ** -0.5),
                               z(pad_q, D_MODEL)], 0) for h in range(M_HEADS)]
    w_kt = jnp.concatenate(k_rows, 0)
    w_gt = jnp.concatenate([w_in[:, o_g:o_cq].T, z(16 - 2 * M_HEADS, D_MODEL)], 0)
    dq = A_NOPE + A_ROPE
    q_tiles, q_sw_tiles, uk_tiles, uv_tiles = [], [], [], []
    zq = lambda c: z(Q_LORA, c)
    for h in range(A_HEADS):
        nope = w_uq[:, h * dq:h * dq + A_NOPE]
        x1 = w_uq[:, h * dq + A_NOPE:h * dq + A_NOPE + half]
        x2 = w_uq[:, h * dq + A_NOPE + half:(h + 1) * dq]
        q_tiles.append(jnp.concatenate([x1, x2, nope, zq(HEAD_TILE - dq)], 1))
        q_sw_tiles.append(jnp.concatenate([x2, x1, zq(HEAD_TILE - A_ROPE)], 1))
        uk_h = w_uk[:, h, :]
        uk_tiles.append(jnp.concatenate([zq(A_ROPE), uk_h, zq(HEAD_TILE - dq)], 1))
        uv_h = w_uv[:, h, :]
        uv_tiles.append(jnp.concatenate([uv_h, zq(HEAD_TILE - A_DV)], 1) if h % 2 == 0
                        else jnp.concatenate([zq(HEAD_TILE - A_DV), uv_h], 1))
    bf = lambda a: a.astype(BF16)
    return {
        "w_main": bf(w_main), "w_kt": bf(w_kt), "w_gt": bf(w_gt),
        "bg_row": b_gates.reshape(1, -1), "bg_col": b_gates.reshape(-1, 1),
        "q_norm_g": q_norm_g.reshape(1, -1), "kv_norm_g": kv_norm_g.reshape(1, -1),
        "w_q2t": bf(jnp.concatenate(q_tiles + q_sw_tiles, 1).T),
        "w_uk_tiles": bf(jnp.concatenate(uk_tiles, 1)),
        "w_uvt": bf(jnp.concatenate(uv_tiles, 1).T),
        "w_uk_heads": bf(jnp.stack(uk_tiles)),
        "w_uv_heads": bf(jnp.stack(uv_tiles)),
        "mlstm_norm_g": mlstm_norm_g.reshape(1, -1), "w_out": bf(w_out),
    }


def _rope_tables(pos):
    half = A_ROPE // 2
    inv = ROPE_BASE ** (-jnp.arange(half, dtype=F32) / half)
    ang = pos.astype(F32)[:, None] * inv[None, :]
    cos, sin = jnp.cos(ang), jnp.sin(ang)
    n = pos.shape[0]
    cos_t = jnp.concatenate([cos, cos, jnp.ones((n, A_NOPE), F32), jnp.zeros((n, HEAD_TILE - A_NOPE - A_ROPE), F32)], 1)
    sin_t = jnp.concatenate([-sin, sin, jnp.zeros((n, HEAD_TILE - A_ROPE), F32)], 1)
    return cos_t, sin_t


def kernel(x_prompt, x_sample, cache_kv_latent, cache_k_rope, state_mlstm_C, state_mlstm_n, state_mlstm_m, page_table, meta_tokens, ffn1_w1, ffn1_w3, ffn1_w2, ln1_g, ln1_b, w_in, b_gates, mlstm_norm_g, q_norm_g, w_uq, kv_norm_g, w_uk, w_uv, w_out, ln2_g, ln2_b, ffn2_w1, ffn2_w3, ffn2_w2, ln3_g, ln3_b):
    assert w_in.shape[0] == DEPTH == 1
    B, S, _ = x_prompt.shape
    Bd, Td, _ = x_sample.shape
    n_real, n_dec = B * S, Bd * Td
    l = 0
    vec = lambda a: a[l].reshape(1, -1)
    wts = _pack_weights(w_in[l], b_gates[l], mlstm_norm_g[l], q_norm_g[l], w_uq[l], kv_norm_g[l],
                        w_uk[l], w_uv[l], w_out[l])
    wts.update({"ln2_g": vec(ln2_g), "ln2_b": vec(ln2_b), "ln3_g": vec(ln3_g), "ln3_b": vec(ln3_b),
                "ffn2_w1": ffn2_w1[l].astype(BF16), "ffn2_w3": ffn2_w3[l].astype(BF16),
                "ffn2_w2": ffn2_w2[l].astype(BF16)})
    f1 = (ffn1_w1[l].astype(BF16), ffn1_w3[l].astype(BF16), ffn1_w2[l].astype(BF16), vec(ln1_g), vec(ln1_b))

    x_real = x_prompt.reshape(n_real, D_MODEL)
    x_extra = jnp.concatenate([meta_tokens.astype(F32), x_sample.reshape(n_dec, D_MODEL)], 0)
    n_extra = x_extra.shape[0]
    tm = 512

    h_real = _ffn_ln(x_real, *f1, tm)
    h_extra = _ffn_ln(x_extra, *f1, n_extra)

    pos_e = jnp.concatenate([jnp.arange(N_META),
                             jnp.tile(page_table.shape[1] * PAGE_SIZE + jnp.arange(Td), Bd)])
    (q_r, kt_r, v_r, om_r, g_r, gt_r, aqt_r, ak_r, avt_r, ckv_r, kr_r) = _mixer_inputs(
        h_real, wts, _rope_tables(N_META + jnp.arange(S)), tm, S // tm)
    (q_e, kt_e, v_e, om_e, g_e, gt_e, aqt_e, ak_e, avt_e, ckv_e, kr_e) = _mixer_inputs(
        h_extra, wts, _rope_tables(pos_e), n_extra, 1)

    cs, ms = (1, M_HEADS, HEAD_TILE, 2 * LANES), (1, M_HEADS, 8, LANES)
    _, c_meta, m_meta = _mlstm_scan(q_e[:N_META], kt_e[:, :N_META], v_e[:N_META], g_e[:N_META],
                                    gt_e[:, :N_META], jnp.zeros(cs, F32), jnp.zeros(ms, F32), 1, N_META, N_META)
    hm_r, c_p, m_p = _mlstm_scan(q_r, kt_r, v_r, g_r, gt_r, c_meta, m_meta, B, S, MLSTM_CHUNK)
    mlstm_c_p = c_p[:, :, :M_DQK, :M_DV][None]
    mlstm_n_p = c_p[:, :, :M_DQK, M_DV][None]
    mlstm_m_p = m_p[:, :, 0, 0][None]

    G = Bd * M_HEADS
    def dec_heads(a, w, used):
        return a.reshape(Bd, Td, M_HEADS, w)[..., :used].transpose(0, 2, 1, 3).reshape(G, Td, used)
    q_d = dec_heads(q_e[N_META:], HEAD_TILE, M_DQK)
    v_d = dec_heads(v_e[N_META:], M_DV, M_DV)
    kt_d = (kt_e[:, N_META:].reshape(M_HEADS, HEAD_TILE, Bd, Td)[:, :M_DQK]
            .transpose(2, 0, 1, 3).reshape(G, M_DQK, Td))
    k_d = kt_d.transpose(0, 2, 1)
    gdec = g_e[N_META:].reshape(Bd, Td, 2, M_HEADS).transpose(0, 3, 1, 2).reshape(G, Td, 2)
    hm_d, c_d, n_d, m_d = _mlstm_step(
        q_d, k_d, kt_d, v_d, gdec, gdec.transpose(0, 2, 1),
        state_mlstm_C[l].reshape(G, M_DQK, M_DV), state_mlstm_n[l].reshape(G, 1, M_DQK),
        state_mlstm_m[l].reshape(G, 1, 1), 64)
    hm_d = hm_d.reshape(Bd, M_HEADS, Td, M_DV).transpose(0, 2, 1, 3).reshape(n_dec, M_HEADS * M_DV)

    attn_r = _attention_prompt(aqt_r, ak_r, avt_r, ak_e[:, :N_META], avt_e[:, :, :N_META], B, S)

    aqt_d = aqt_e[:, :, N_META:]
    q_lat_t = _absorb_q(aqt_d, wts["w_uk_heads"])
    rows = A_HEADS * Td
    by_seq = lambda a: (a.reshape(A_HEADS, a.shape[1], Bd, Td).transpose(2, 0, 3, 1)
                        .reshape(Bd, rows, a.shape[1]))
    o_lat = _decode_attention(
        page_table, by_seq(q_lat_t), by_seq(aqt_d[:, :A_ROPE]),
        ckv_e[N_META:].reshape(Bd, Td, KV_LORA), kr_e[N_META:].reshape(Bd, Td, A_ROPE),
        cache_kv_latent[l], jnp.swapaxes(cache_k_rope[l], 1, 2))
    o_lat = o_lat.reshape(Bd, A_HEADS, Td, KV_LORA).transpose(1, 0, 2, 3).reshape(A_HEADS, n_dec, KV_LORA)
    attn_d = _lat2attn(o_lat, wts["w_uv_heads"])

    y_real = _mixer_output_ffn(h_real, hm_r, om_r, attn_r, wts, tm)
    y_dec = _mixer_output_ffn(h_extra[N_META:], hm_d, om_e[N_META:], attn_d, wts, n_dec)

    bc = lambda a: jnp.broadcast_to(a[None], (B,) + a.shape)
    kv_p = jnp.concatenate([bc(ckv_e[:N_META]), ckv_r.reshape(B, S, KV_LORA)], 1)[None]
    kr_p = jnp.concatenate([bc(kr_e[:N_META]), kr_r.reshape(B, S, A_ROPE)], 1)[None]
    return (y_real.reshape(B, S, D_MODEL), y_dec.reshape(Bd, Td, D_MODEL), kv_p, kr_p,
            mlstm_c_p, mlstm_n_p, mlstm_m_p,
            ckv_e[N_META:].reshape(1, Bd, Td, KV_LORA), kr_e[N_META:].reshape(1, Bd, Td, A_ROPE),
            c_d.reshape(1, Bd, M_HEADS, M_DQK, M_DV), n_d.reshape(1, Bd, M_HEADS, M_DQK),
            m_d.reshape(1, Bd, M_HEADS))
```

```python
import functools
import math

import numpy as np
import jax
import jax.numpy as jnp
from jax import lax
from jax.experimental import pallas as pl
from jax.experimental.pallas import tpu as pltpu

F32 = jnp.float32
BF16 = jnp.bfloat16

D_MODEL = 1024
N_META = 16
M_HEADS = 4
M_DQK = 64
M_DV = 128
A_HEADS = 8
A_NOPE = 64
A_ROPE = 32
A_DV = 64
Q_LORA = 256
KV_LORA = 256
ROPE_BASE = 10000.0
PAGE_SIZE = 128
D_FF = 2816
DEPTH = 1
ALPHA = (2.0 * DEPTH) ** 0.25
LN_EPS = 1e-5
RMS_EPS = 1e-6
ATTN_SCALE = (A_NOPE + A_ROPE) ** -0.5
LOG2E = math.log2(math.e)

LANES = 128
HEAD_TILE = LANES
FF_CHUNK = 256
MLSTM_CHUNK = 256
ATTN_BLOCK = 1024
V_ROWS = 80
DEC_SUB = 8
DEC_CHUNK = 32
DEC_SLOTS = 4
VMEM_LIMIT = 56 * 1024 * 1024

_C_Q = 0
_C_V = _C_Q + M_HEADS * HEAD_TILE
_C_O = _C_V + M_HEADS * M_DV
_C_CQ = _C_O + M_HEADS * M_DV
_C_CKV = _C_CQ + Q_LORA
_C_G = _C_CKV + KV_LORA
_C_KRA = _C_G + LANES
_C_KRB = _C_KRA + LANES
_C_END = _C_KRB + LANES


def _cparams(sem):
    return pltpu.CompilerParams(dimension_semantics=sem, vmem_limit_bytes=VMEM_LIMIT)


def _const_spec(shape):
    nd = len(shape)
    return pl.BlockSpec(shape, lambda *_: (0,) * nd, pipeline_mode=pl.Buffered(1))


def _layer_norm(r, g, b):
    mu = jnp.mean(r, axis=-1, keepdims=True)
    d = r - mu
    var = jnp.mean(d * d, axis=-1, keepdims=True)
    return d * lax.rsqrt(var + LN_EPS) * g + b


def _rms_norm(x, g):
    return x * lax.rsqrt(jnp.mean(x * x, axis=-1, keepdims=True) + RMS_EPS) * g


def _dot(a, b):
    return jnp.dot(a, b, preferred_element_type=F32)


def _dot_nt(a, b):
    return lax.dot_general(a, b, (((1,), (1,)), ((), ())), preferred_element_type=F32)


def _swiglu_ln(x, w1_ref, w3_ref, w2_ref, g, b):
    xb = x.astype(BF16)
    y = None
    for j in range(D_FF // FF_CHUNK):
        sl = slice(j * FF_CHUNK, (j + 1) * FF_CHUNK)
        a = _dot(xb, w1_ref[:, sl])
        c = _dot(xb, w3_ref[:, sl])
        hj = (a * jax.nn.sigmoid(a) * c).astype(BF16)
        yj = _dot(hj, w2_ref[sl, :])
        y = yj if y is None else y + yj
    return _layer_norm(ALPHA * x + 0.5 * y, g, b)


def _ffn_kernel(x_ref, w1_ref, w3_ref, w2_ref, g_ref, b_ref, o_ref):
    o_ref[...] = _swiglu_ln(x_ref[...], w1_ref, w3_ref, w2_ref, g_ref[...], b_ref[...])


def _ffn_ln(x, w1, w3, w2, g, b, tm):
    n = x.shape[0]
    row = pl.BlockSpec((tm, D_MODEL), lambda i: (i, 0))
    return pl.pallas_call(
        _ffn_kernel,
        grid=(n // tm,),
        in_specs=[row, _const_spec(w1.shape), _const_spec(w3.shape), _const_spec(w2.shape),
                  _const_spec(g.shape), _const_spec(b.shape)],
        out_specs=row,
        out_shape=jax.ShapeDtypeStruct((n, D_MODEL), F32),
        compiler_params=_cparams(("parallel",)),
        name="ffn_ln",
    )(x, w1, w3, w2, g, b)


def _mixin_kernel(x_ref, w_ref, wt_ref, wgt_ref, bg_ref, bgt_ref, qg_ref, wqt_ref, kvg_ref, wuk_ref, wuvt_ref,
                  cos_ref, sin_ref, cost_ref, sint_ref,
                  q_ref, kt_ref, v_ref, o_ref, g_ref, gt_ref, aqt_ref, ak_ref, avt_ref, ckv_ref, kr_ref):
    xb = x_ref[...].astype(BF16)
    z = _dot(xb, w_ref[...])
    q_ref[...] = z[:, _C_Q:_C_V].astype(BF16)
    v_ref[...] = z[:, _C_V:_C_O].astype(BF16)
    o_ref[...] = z[:, _C_O:_C_CQ]
    def gates(pre, is_forget):
        logsig = jnp.minimum(pre, 0.0) - jnp.log1p(jnp.exp(-jnp.abs(pre)))
        return jnp.where(is_forget, logsig, pre)
    gc = z[:, _C_G:_C_G + 2 * M_HEADS] + bg_ref[...]
    g_ref[...] = gates(gc, lax.broadcasted_iota(jnp.int32, gc.shape, 1) >= M_HEADS)
    gr = _dot_nt(wgt_ref[...], xb)[:2 * M_HEADS] + bgt_ref[...]
    gt_ref[...] = gates(gr, lax.broadcasted_iota(jnp.int32, gr.shape, 0) >= M_HEADS)
    kt_ref[...] = _dot_nt(wt_ref[...], xb).astype(BF16)
    cq = _rms_norm(z[:, _C_CQ:_C_CKV], qg_ref[...]).astype(BF16)
    qat = _dot_nt(wqt_ref[...], cq)
    ckv = _rms_norm(z[:, _C_CKV:_C_G], kvg_ref[...])
    ckv_ref[...] = ckv
    ckv_b = ckv.astype(BF16)
    kn = _dot(ckv_b, wuk_ref[...])
    vt = _dot_nt(wuvt_ref[...], ckv_b)
    kr = z[:, _C_KRA:_C_KRB] * cos_ref[...] + z[:, _C_KRB:_C_END] * sin_ref[...]
    kr_ref[...] = kr[:, :A_ROPE]
    cost = cost_ref[...]
    sint = sint_ref[...]
    ones_row = (lax.broadcasted_iota(jnp.int32, (V_ROWS, 1), 0) == A_DV).astype(F32)
    nh = A_HEADS * HEAD_TILE
    for h in range(A_HEADS):
        sl = slice(h * HEAD_TILE, (h + 1) * HEAD_TILE)
        sw = slice(nh + h * HEAD_TILE, nh + (h + 1) * HEAD_TILE)
        aqt_ref[h] = ((qat[sl] * cost + qat[sw] * sint) * (ATTN_SCALE * LOG2E)).astype(BF16)
        ak_ref[h] = (kn[:, sl] + kr).astype(BF16)
        avt_ref[h] = (vt[h * V_ROWS:(h + 1) * V_ROWS] + ones_row).astype(BF16)


def _mixer_inputs(x, wts, tables, tm, n_tab_blocks):
    n = x.shape[0]
    cos, sin = tables
    row = lambda w: pl.BlockSpec((tm, w), lambda i: (i, 0))
    colb = lambda r: pl.BlockSpec((r, tm), lambda i: (0, i))
    head = pl.BlockSpec((A_HEADS, tm, HEAD_TILE), lambda i: (0, i, 0))
    head_t = lambda r: pl.BlockSpec((A_HEADS, r, tm), lambda i: (0, 0, i))
    tab = pl.BlockSpec((tm, HEAD_TILE), lambda i: (i % n_tab_blocks, 0))
    tab_t = pl.BlockSpec((HEAD_TILE, tm), lambda i: (0, i % n_tab_blocks))
    consts = [wts["w_main"], wts["w_kt"], wts["w_gt"], wts["bg_row"], wts["bg_col"], wts["q_norm_g"],
              wts["w_q2t"], wts["kv_norm_g"], wts["w_uk_tiles"], wts["w_uvt"]]
    hq = M_HEADS * HEAD_TILE
    hv = M_HEADS * M_DV
    out_shape = (
        jax.ShapeDtypeStruct((n, hq), BF16),
        jax.ShapeDtypeStruct((hq, n), BF16),
        jax.ShapeDtypeStruct((n, hv), BF16),
        jax.ShapeDtypeStruct((n, hv), F32),
        jax.ShapeDtypeStruct((n, 2 * M_HEADS), F32),
        jax.ShapeDtypeStruct((2 * M_HEADS, n), F32),
        jax.ShapeDtypeStruct((A_HEADS, HEAD_TILE, n), BF16),
        jax.ShapeDtypeStruct((A_HEADS, n, HEAD_TILE), BF16),
        jax.ShapeDtypeStruct((A_HEADS, V_ROWS, n), BF16),
        jax.ShapeDtypeStruct((n, KV_LORA), F32),
        jax.ShapeDtypeStruct((n, A_ROPE), F32),
    )
    out_specs = (row(hq), colb(hq), row(hv), row(hv), row(2 * M_HEADS), colb(2 * M_HEADS),
                 head_t(HEAD_TILE), head, head_t(V_ROWS), row(KV_LORA), row(A_ROPE))
    return pl.pallas_call(
        _mixin_kernel,
        grid=(n // tm,),
        in_specs=[row(D_MODEL)] + [_const_spec(c.shape) for c in consts] + [tab, tab, tab_t, tab_t],
        out_specs=out_specs,
        out_shape=out_shape,
        compiler_params=_cparams(("parallel",)),
        name="mixer_inputs",
    )(x, *consts, cos, sin, cos.T, sin.T)


def _split3(x):
    hi = x.astype(BF16)
    r1 = x - hi.astype(F32)
    mid = r1.astype(BF16)
    lo = (r1 - mid.astype(F32)).astype(BF16)
    return hi, mid, lo


def _mlstm_kernel(q_ref, kt_ref, v_ref, g_ref, gt_ref, c0_ref, m0_ref, h_ref, c_ref, m_ref, c_sc, m_sc,
                  *, chunk):
    ci = pl.program_id(1)

    @pl.when(ci == 0)
    def _():
        c_sc[...] = c0_ref[0]
        m_sc[...] = m0_ref[0]

    L = chunk
    row = lax.broadcasted_iota(jnp.int32, (L, L), 0)
    col = lax.broadcasted_iota(jnp.int32, (L, L), 1)
    causal = row >= col
    tril = causal.astype(BF16)
    triu = (row <= col).astype(BF16)
    g = g_ref[...]
    gt = gt_ref[...]
    cum_c = sum(_dot(tril, p) for p in _split3(g))
    cum_r = sum(_dot(p, triu) for p in _split3(gt))
    ones_col = (lax.broadcasted_iota(jnp.int32, (L, LANES), 1) == 0).astype(BF16)
    for h in range(M_HEADS):
        sl = slice(h * HEAD_TILE, (h + 1) * HEAD_TILE)
        qh = q_ref[:, sl]
        kth = kt_ref[sl, :]
        v_aug = jnp.concatenate([v_ref[:, h * M_DV:(h + 1) * M_DV], ones_col], axis=1)
        ig_r = gt[h:h + 1, :]
        lf_r = gt[M_HEADS + h:M_HEADS + h + 1, :]
        cr = cum_r[M_HEADS + h:M_HEADS + h + 1, :]
        cc = cum_c[:, M_HEADS + h:M_HEADS + h + 1]
        m_run = m_sc[h, 0:1, 0:1]
        d = jnp.where(causal, cc - cr + ig_r, -jnp.inf)
        inter = cc + m_run
        m_t = jnp.maximum(inter, jnp.max(d, axis=1, keepdims=True))
        s = (_dot(qh, kth) * jnp.exp(d - m_t)).astype(BF16)
        w_inter = jnp.exp(inter - m_t)
        c_state = c_sc[h]
        r = _dot(s, v_aug) + w_inter * _dot(qh, c_state.astype(BF16))
        qn = r[:, M_DV:M_DV + 1]
        den = jnp.maximum(jnp.abs(qn), jnp.exp(-m_t))
        h_ref[:, h * M_DV:(h + 1) * M_DV] = r[:, :M_DV] / den
        last = jnp.sum(lf_r, axis=1, keepdims=True)
        g_r = last - cr + ig_r
        m_new = jnp.maximum(last + m_run, jnp.max(g_r, axis=1, keepdims=True))
        decay = jnp.exp(last + m_run - m_new)
        wkt = (kth.astype(F32) * jnp.exp(g_r - m_new)).astype(BF16)
        c_sc[h] = decay * c_state + _dot(wkt, v_aug)
        m_sc[h] = jnp.broadcast_to(m_new, (8, LANES))

    @pl.when(ci == pl.num_programs(1) - 1)
    def _():
        c_ref[0] = c_sc[...]
        m_ref[0] = m_sc[...]


def _mlstm_scan(q, kt, v, g, gt, c0, m0, batch, seq, chunk):
    nc = seq // chunk
    hq = M_HEADS * HEAD_TILE
    hv = M_HEADS * M_DV
    row = lambda w: pl.BlockSpec((chunk, w), lambda b, c: (b * nc + c, 0))
    colb = lambda r: pl.BlockSpec((r, chunk), lambda b, c: (0, b * nc + c))
    st_in = lambda s: pl.BlockSpec((1,) + s, lambda b, c: (0,) * (len(s) + 1))
    st_out = lambda s: pl.BlockSpec((1,) + s, lambda b, c: (b,) + (0,) * len(s))
    cs, ms = (M_HEADS, HEAD_TILE, 2 * LANES), (M_HEADS, 8, LANES)
    return pl.pallas_call(
        functools.partial(_mlstm_kernel, chunk=chunk),
        grid=(batch, nc),
        in_specs=[row(hq), colb(hq), row(hv), row(2 * M_HEADS), colb(2 * M_HEADS), st_in(cs), st_in(ms)],
        out_specs=(row(hv), st_out(cs), st_out(ms)),
        out_shape=(jax.ShapeDtypeStruct((batch * seq, hv), F32),
                   jax.ShapeDtypeStruct((batch,) + cs, F32),
                   jax.ShapeDtypeStruct((batch,) + ms, F32)),
        scratch_shapes=[pltpu.VMEM(cs, F32), pltpu.VMEM(ms, F32)],
        compiler_params=_cparams(("parallel", "arbitrary")),
        name="mlstm_scan",
    )(q, kt, v, g, gt, c0, m0)


def _mlstm_step_kernel(q_ref, k_ref, kt_ref, v_ref, gc_ref, gr_ref, c0_ref, n0_ref, m0_ref,
                       h_ref, c_ref, n_ref, m_ref):
    T = q_ref.shape[1]
    row = lax.broadcasted_iota(jnp.int32, (T, T), 0)
    col = lax.broadcasted_iota(jnp.int32, (T, T), 1)
    causal = (row >= col)[None]
    q = q_ref[...]
    k = k_ref[...]
    v = v_ref[...]
    ig_c, lf_c = gc_ref[:, :, 0:1], gc_ref[:, :, 1:2]
    ig_r, lf_r = gr_ref[:, 0:1, :], gr_ref[:, 1:2, :]
    m0 = m0_ref[...]
    c0 = c0_ref[...]
    n0 = n0_ref[...]
    cum_c = jnp.sum(jnp.where(causal, lf_r, 0.0), axis=2, keepdims=True)
    cum_r = jnp.sum(jnp.where((row <= col)[None], lf_c, 0.0), axis=1, keepdims=True)
    d = jnp.where(causal, cum_c - cum_r + ig_r, -jnp.inf)
    inter = cum_c + m0
    m_t = jnp.maximum(inter, jnp.max(d, axis=2, keepdims=True))
    qk = jnp.einsum("gtd,gsd->gts", q, k, preferred_element_type=F32)
    s = qk * jnp.exp(d - m_t)
    w_inter = jnp.exp(inter - m_t)
    num = (jnp.einsum("gts,gsv->gtv", s.astype(BF16), v, preferred_element_type=F32)
           + w_inter * jnp.einsum("gtd,gdv->gtv", q, c0.astype(BF16), preferred_element_type=F32))
    qn = jnp.sum(s, axis=2, keepdims=True) + w_inter * jnp.sum(q.astype(F32) * n0, axis=2, keepdims=True)
    h_ref[...] = num / jnp.maximum(jnp.abs(qn), jnp.exp(-m_t))
    last = cum_c[:, T - 1:T, :]
    g_c = last - cum_c + ig_c
    g_r = last - cum_r + ig_r
    m_new = jnp.maximum(last + m0, jnp.max(g_c, axis=1, keepdims=True))
    decay = jnp.exp(last + m0 - m_new)
    wk = k.astype(F32) * jnp.exp(g_c - m_new)
    wkt = (kt_ref[...].astype(F32) * jnp.exp(g_r - m_new)).astype(BF16)
    c_ref[...] = decay * c0 + jnp.einsum("gds,gsv->gdv", wkt, v, preferred_element_type=F32)
    n_ref[...] = decay * n0 + jnp.sum(wk, axis=1, keepdims=True)
    m_ref[...] = m_new


def _mlstm_step(q, k, kt, v, gc, gr, c0, n0, m0, gb):
    gtot, T = q.shape[0], q.shape[1]
    blk = lambda a, b: pl.BlockSpec((gb, a, b), lambda i: (i, 0, 0))
    return pl.pallas_call(
        _mlstm_step_kernel,
        grid=(gtot // gb,),
        in_specs=[blk(T, M_DQK), blk(T, M_DQK), blk(M_DQK, T), blk(T, M_DV), blk(T, 2), blk(2, T),
                  blk(M_DQK, M_DV), blk(1, M_DQK), blk(1, 1)],
        out_specs=(blk(T, M_DV), blk(M_DQK, M_DV), blk(1, M_DQK), blk(1, 1)),
        out_shape=(jax.ShapeDtypeStruct((gtot, T, M_DV), F32),
                   jax.ShapeDtypeStruct((gtot, M_DQK, M_DV), F32),
                   jax.ShapeDtypeStruct((gtot, 1, M_DQK), F32),
                   jax.ShapeDtypeStruct((gtot, 1, 1), F32)),
        compiler_params=_cparams(("parallel",)),
        name="mlstm_step",
    )(q, k, kt, v, gc, gr, c0, n0, m0)


def _attn_kernel(qi_tab, ki_tab, qt_ref, k_ref, vt_ref, km_ref, vmt_ref, o_ref, acc_sc, m_sc):
    step = pl.program_id(1)
    qi = qi_tab[step]
    ki = ki_tab[step]
    tq = qt_ref.shape[2]

    @pl.when(ki == 0)
    def _():
        ss = [_dot(km_ref[h], qt_ref[h]) for h in range(A_HEADS)]
        ms = [jnp.max(s, axis=0, keepdims=True) for s in ss]
        for h in range(A_HEADS):
            acc_sc[h] = _dot(vmt_ref[h], jnp.exp2(ss[h] - ms[h]).astype(BF16))
            m_sc[h] = ms[h]

    def update(masked):
        if masked:
            row = lax.broadcasted_iota(jnp.int32, (tq, tq), 0)
            col = lax.broadcasted_iota(jnp.int32, (tq, tq), 1)
            keep = row <= col
        s_next = _dot(k_ref[0], qt_ref[0])
        for h in range(A_HEADS):
            s = s_next
            if h + 1 < A_HEADS:
                s_next = _dot(k_ref[h + 1], qt_ref[h + 1])
            if masked:
                s = jnp.where(keep, s, -jnp.inf)
            m_prev = m_sc[h]
            m_new = jnp.maximum(m_prev, jnp.max(s, axis=0, keepdims=True))
            p = jnp.exp2(s - m_new).astype(BF16)
            acc_sc[h] = jnp.exp2(m_prev - m_new) * acc_sc[h] + _dot(vt_ref[h], p)
            m_sc[h] = m_new

    @pl.when(ki < qi)
    def _():
        update(False)

    @pl.when(ki == qi)
    def _():
        update(True)
        for j in range(A_HEADS // 2):
            ae = acc_sc[2 * j]
            ao = acc_sc[2 * j + 1]
            pair = jnp.concatenate([ae[:A_DV] / ae[A_DV:A_DV + 1], ao[:A_DV] / ao[A_DV:A_DV + 1]], axis=0)
            o_ref[:, j * HEAD_TILE:(j + 1) * HEAD_TILE] = pair.T.astype(o_ref.dtype)


def _attention_prompt(aqt, ak, avt, k_meta, vt_meta, batch, seq):
    blk = ATTN_BLOCK
    nq = seq // blk
    qi_np, ki_np = zip(*[(i, j) for i in range(nq) for j in range(i + 1)])
    qi_tab = jnp.asarray(np.array(qi_np, np.int32))
    ki_tab = jnp.asarray(np.array(ki_np, np.int32))
    rows = lambda tab: pl.BlockSpec((A_HEADS, blk, HEAD_TILE),
                                    lambda b, s, qt, kt: (0, b * nq + (qt, kt)[tab][s], 0))
    cols = lambda tab, r: pl.BlockSpec((A_HEADS, r, blk),
                                       lambda b, s, qt, kt: (0, 0, b * nq + (qt, kt)[tab][s]))
    grid_spec = pltpu.PrefetchScalarGridSpec(
        num_scalar_prefetch=2,
        grid=(batch, len(qi_np)),
        in_specs=[cols(0, HEAD_TILE), rows(1), cols(1, V_ROWS),
                  pl.BlockSpec((A_HEADS, N_META, HEAD_TILE), lambda b, s, qt, kt: (0, 0, 0)),
                  pl.BlockSpec((A_HEADS, V_ROWS, N_META), lambda b, s, qt, kt: (0, 0, 0))],
        out_specs=pl.BlockSpec((blk, A_HEADS * A_DV), lambda b, s, qt, kt: (b * nq + qt[s], 0)),
        scratch_shapes=[pltpu.VMEM((A_HEADS, V_ROWS, blk), F32), pltpu.VMEM((A_HEADS, 1, blk), F32)],
    )
    return pl.pallas_call(
        _attn_kernel,
        grid_spec=grid_spec,
        out_shape=jax.ShapeDtypeStruct((batch * seq, A_HEADS * A_DV), BF16),
        compiler_params=_cparams(("parallel", "arbitrary")),
        name="attn_prompt",
    )(qi_tab, ki_tab, aqt, ak, avt, k_meta, vt_meta)


def _absorb_kernel(qt_ref, wuk_ref, o_ref):
    for h in range(A_HEADS):
        o_ref[h] = _dot(wuk_ref[h], qt_ref[h]).astype(BF16)


def _absorb_q(aqt, wuk_tiles):
    n = aqt.shape[2]
    return pl.pallas_call(
        _absorb_kernel,
        out_shape=jax.ShapeDtypeStruct((A_HEADS, KV_LORA, n), BF16),
        compiler_params=pltpu.CompilerParams(vmem_limit_bytes=VMEM_LIMIT),
        name="absorb_q",
    )(aqt, wuk_tiles)


def _decode_attn_kernel(pt_ref, ql_ref, qr_ref, cn_ref, kn_ref, lat_hbm, krt_hbm, o_ref,
                        lat_buf, krt_buf, sem_lat, sem_krt):
    b = pl.program_id(0)
    nb = pl.num_programs(0)
    sub, chunk, n_slots = DEC_SUB, DEC_CHUNK, DEC_SLOTS
    n_sub = pt_ref.shape[1] // sub
    spc = chunk // sub
    n_chunks = n_sub // spc
    ahead = 2 * spc
    assert n_chunks % n_slots == 0 and n_slots >= 4

    def copies(bb, j):
        slot = (j // spc) % n_slots
        for k in range((j % spc) * sub, (j % spc + 1) * sub):
            page = pt_ref[bb, (j // spc) * chunk + k]
            yield pltpu.make_async_copy(lat_hbm.at[page], lat_buf.at[slot, k], sem_lat.at[slot])
            yield pltpu.make_async_copy(krt_hbm.at[page], krt_buf.at[slot, k], sem_krt.at[slot])

    def start(bb, j):
        for cp in copies(bb, j):
            cp.start()

    def wait_chunk(c):
        for j in range(c * spc, (c + 1) * spc):
            for cp in copies(b, j):
                cp.wait()

    @pl.when(b == 0)
    def _():
        for j in range(ahead):
            start(0, j)

    ql = ql_ref[0]
    qr = qr_ref[0]
    rows = ql.shape[0]
    m_run = jnp.full((rows, 1), -jnp.inf, F32)
    l_run = jnp.zeros((rows, 1), F32)
    acc = jnp.zeros((rows, KV_LORA), F32)

    def scores(j):
        slot = (j // spc) % n_slots
        ks = range((j % spc) * sub, (j % spc + 1) * sub)
        lats = [lat_buf[slot, k].astype(BF16) for k in ks]
        s = jnp.concatenate(
            [_dot_nt(ql, lat) + _dot(qr, krt_buf[slot, k].astype(BF16)) for k, lat in zip(ks, lats)],
            axis=1)
        return lats, s

    wait_chunk(0)
    nxt = scores(0)
    for i in range(n_sub):
        lats, s = nxt
        if i + 1 < n_sub:
            if (i + 1) % spc == 0:
                wait_chunk((i + 1) // spc)
            nxt = scores(i + 1)
        j = i + ahead
        if j < n_sub:
            start(b, j)
        else:
            @pl.when(b + 1 < nb)
            def _():
                start(b + 1, j - n_sub)
        m_new = jnp.maximum(m_run, jnp.max(s, axis=1, keepdims=True))
        p = jnp.exp2(s - m_new)
        alpha = jnp.exp2(m_run - m_new)
        l_run = alpha * l_run + jnp.sum(p, axis=1, keepdims=True)
        pb = p.astype(BF16)
        pv = sum(_dot(pb[:, k * PAGE_SIZE:(k + 1) * PAGE_SIZE], lat) for k, lat in enumerate(lats))
        acc = alpha * acc + pv
        m_run = m_new

    T = cn_ref.shape[1]
    cn = cn_ref[0].astype(BF16).astype(F32)
    kn = kn_ref[0].astype(BF16).astype(F32)
    qlf = ql.astype(F32)
    qrf = qr.astype(F32)
    t_of_row = lax.broadcasted_iota(jnp.int32, (rows, 1), 0) % T
    s_new = []
    for t in range(T):
        st = (jnp.sum(qlf * cn[t:t + 1], axis=1, keepdims=True)
              + jnp.sum(qrf * kn[t:t + 1], axis=1, keepdims=True))
        s_new.append(jnp.where(t_of_row >= t, st, -jnp.inf))
    m_fin = m_run
    for st in s_new:
        m_fin = jnp.maximum(m_fin, st)
    alpha = jnp.exp2(m_run - m_fin)
    acc = alpha * acc
    l_run = alpha * l_run
    for t, st in enumerate(s_new):
        pt = jnp.exp2(st - m_fin)
        l_run = l_run + pt
        acc = acc + pt * cn[t:t + 1]
    o_ref[0] = acc / l_run


def _decode_attention(page_table, q_lat, q_rope, c_new, kr_new, cache_lat, cache_krt):
    nb = page_table.shape[0]
    rows = q_lat.shape[1]
    per_b = lambda a, b: pl.BlockSpec((1, a, b), lambda i, pt: (i, 0, 0))
    hbm = pl.BlockSpec(memory_space=pl.ANY)
    grid_spec = pltpu.PrefetchScalarGridSpec(
        num_scalar_prefetch=1,
        grid=(nb,),
        in_specs=[per_b(rows, KV_LORA), per_b(rows, A_ROPE), per_b(c_new.shape[1], KV_LORA),
                  per_b(kr_new.shape[1], A_ROPE), hbm, hbm],
        out_specs=per_b(rows, KV_LORA),
        scratch_shapes=[pltpu.VMEM((DEC_SLOTS, DEC_CHUNK, PAGE_SIZE, KV_LORA), F32),
                        pltpu.VMEM((DEC_SLOTS, DEC_CHUNK, A_ROPE, PAGE_SIZE), F32),
                        pltpu.SemaphoreType.DMA((DEC_SLOTS,)), pltpu.SemaphoreType.DMA((DEC_SLOTS,))],
    )
    return pl.pallas_call(
        _decode_attn_kernel,
        grid_spec=grid_spec,
        out_shape=jax.ShapeDtypeStruct((nb, rows, KV_LORA), F32),
        compiler_params=_cparams(("arbitrary",)),
        name="attn_decode",
    )(page_table, q_lat, q_rope, c_new, kr_new, cache_lat, cache_krt)


def _lat2attn_kernel(o_ref, wuv_ref, a_ref):
    for j in range(A_HEADS // 2):
        pair = (_dot(o_ref[2 * j].astype(BF16), wuv_ref[2 * j])
                + _dot(o_ref[2 * j + 1].astype(BF16), wuv_ref[2 * j + 1]))
        a_ref[:, j * HEAD_TILE:(j + 1) * HEAD_TILE] = pair.astype(a_ref.dtype)


def _lat2attn(o_lat, wuv_tiles):
    n = o_lat.shape[1]
    return pl.pallas_call(
        _lat2attn_kernel,
        out_shape=jax.ShapeDtypeStruct((n, A_HEADS * A_DV), BF16),
        compiler_params=pltpu.CompilerParams(vmem_limit_bytes=VMEM_LIMIT),
        name="lat2attn",
    )(o_lat, wuv_tiles)


def _mixout_kernel(x_ref, hm_ref, om_ref, at_ref, ng_ref, wo_ref, g2_ref, b2_ref,
                   w1_ref, w3_ref, w2_ref, g3_ref, b3_ref, o_ref):
    x = x_ref[...]
    parts = []
    for h in range(M_HEADS):
        sl = slice(h * M_DV, (h + 1) * M_DV)
        hn = _rms_norm(hm_ref[:, sl], ng_ref[:, sl])
        parts.append((hn * jax.nn.sigmoid(om_ref[:, sl])).astype(BF16))
    cat = jnp.concatenate(parts + [at_ref[...]], axis=1)
    x2 = _layer_norm(ALPHA * x + _dot(cat, wo_ref[...]), g2_ref[...], b2_ref[...])
    o_ref[...] = _swiglu_ln(x2, w1_ref, w3_ref, w2_ref, g3_ref[...], b3_ref[...])


def _mixer_output_ffn(x, hm, om, attn, wts, tm):
    n = x.shape[0]
    row = lambda w: pl.BlockSpec((tm, w), lambda i: (i, 0))
    consts = [wts["mlstm_norm_g"], wts["w_out"], wts["ln2_g"], wts["ln2_b"],
              wts["ffn2_w1"], wts["ffn2_w3"], wts["ffn2_w2"], wts["ln3_g"], wts["ln3_b"]]
    hv = M_HEADS * M_DV
    return pl.pallas_call(
        _mixout_kernel,
        grid=(n // tm,),
        in_specs=[row(D_MODEL), row(hv), row(hv), row(A_HEADS * A_DV)] + [_const_spec(c.shape) for c in consts],
        out_specs=row(D_MODEL),
        out_shape=jax.ShapeDtypeStruct((n, D_MODEL), F32),
        compiler_params=_cparams(("parallel",)),
        name="mixer_output_ffn",
    )(x, hm, om, attn, *consts)


def _pack_weights(w_in, b_gates, mlstm_norm_g, q_norm_g, w_uq, kv_norm_g, w_uk, w_uv, w_out):
    o_q, o_k, o_v, o_o = 0, M_HEADS * M_DQK, 2 * M_HEADS * M_DQK, 2 * M_HEADS * M_DQK + M_HEADS * M_DV
    o_g = o_o + M_HEADS * M_DV
    o_cq = o_g + 2 * M_HEADS
    o_ckv = o_cq + Q_LORA
    o_kr = o_ckv + KV_LORA
    half = A_ROPE // 2
    dq = A_NOPE + A_ROPE

    def pad_axis(a, axis, lo, size):
        pads = [(0, 0)] * a.ndim
        pads[axis] = (lo, size - lo - a.shape[axis])
        return jnp.pad(a, pads)

    q_cols = pad_axis(w_in[:, o_q:o_k].reshape(D_MODEL, M_HEADS, M_DQK), 2, 0, HEAD_TILE).reshape(D_MODEL, -1)
    kr = w_in[:, o_kr:o_kr + A_ROPE]
    kr_sw = jnp.concatenate([kr[:, half:], kr[:, :half]], 1)
    w_main = jnp.concatenate(
        [q_cols, w_in[:, o_v:o_g], w_in[:, o_cq:o_kr], pad_axis(w_in[:, o_g:o_cq], 1, 0, LANES),
         pad_axis(kr, 1, 0, LANES), pad_axis(kr_sw, 1, 0, LANES)], 1)
    assert w_main.shape[1] == _C_END
    w_kt = pad_axis((w_in[:, o_k:o_v].T * (M_DQK ** -0.5)).reshape(M_HEADS, M_DQK, D_MODEL), 1, 0, HEAD_TILE)
    w_kt = w_kt.reshape(M_HEADS * HEAD_TILE, D_MODEL)
    w_gt = pad_axis(w_in[:, o_g:o_cq].T, 0, 0, 16)
    uq = w_uq.reshape(Q_LORA, A_HEADS, dq)
    q_tiles = pad_axis(jnp.concatenate([uq[..., A_NOPE:], uq[..., :A_NOPE]], -1), 2, 0, HEAD_TILE)
    q_sw_tiles = pad_axis(jnp.concatenate([uq[..., A_NOPE + half:], uq[..., A_NOPE:A_NOPE + half]], -1),
                          2, 0, HEAD_TILE)
    w_q2 = jnp.concatenate([q_tiles.reshape(Q_LORA, -1), q_sw_tiles.reshape(Q_LORA, -1)], 1)
    uk_tiles = pad_axis(w_uk, 2, A_ROPE, HEAD_TILE)
    w_uvt = pad_axis(w_uv.transpose(1, 2, 0), 1, 0, V_ROWS).reshape(A_HEADS * V_ROWS, KV_LORA)
    uv_pairs = w_uv.reshape(KV_LORA, A_HEADS // 2, 2, A_DV)
    uv_heads = jnp.stack([pad_axis(uv_pairs[:, :, 0], 2, 0, HEAD_TILE),
                          pad_axis(uv_pairs[:, :, 1], 2, A_DV, HEAD_TILE)], 2)
    bf = lambda a: a.astype(BF16)
    return {
        "w_main": bf(w_main), "w_kt": bf(w_kt), "w_gt": bf(w_gt),
        "bg_row": b_gates.reshape(1, -1), "bg_col": b_gates.reshape(-1, 1),
        "q_norm_g": q_norm_g.reshape(1, -1), "kv_norm_g": kv_norm_g.reshape(1, -1),
        "w_q2t": bf(w_q2.T),
        "w_uk_tiles": bf(uk_tiles.reshape(KV_LORA, -1)),
        "w_uvt": bf(w_uvt),
        "w_uk_heads": bf(uk_tiles.transpose(1, 0, 2)),
        "w_uv_heads": bf(uv_heads.reshape(KV_LORA, A_HEADS, HEAD_TILE).transpose(1, 0, 2)),
        "mlstm_norm_g": mlstm_norm_g.reshape(1, -1), "w_out": bf(w_out),
    }


def _rope_tables(pos):
    half = A_ROPE // 2
    inv = ROPE_BASE ** (-jnp.arange(half, dtype=F32) / half)
    ang = pos.astype(F32)[:, None] * inv[None, :]
    cos, sin = jnp.cos(ang), jnp.sin(ang)
    n = pos.shape[0]
    cos_t = jnp.concatenate([cos, cos, jnp.ones((n, A_NOPE), F32), jnp.zeros((n, HEAD_TILE - A_NOPE - A_ROPE), F32)], 1)
    sin_t = jnp.concatenate([-sin, sin, jnp.zeros((n, HEAD_TILE - A_ROPE), F32)], 1)
    return cos_t, sin_t


def kernel(x_prompt, x_sample, cache_kv_latent, cache_k_rope, state_mlstm_C, state_mlstm_n, state_mlstm_m, page_table, meta_tokens, ffn1_w1, ffn1_w3, ffn1_w2, ln1_g, ln1_b, w_in, b_gates, mlstm_norm_g, q_norm_g, w_uq, kv_norm_g, w_uk, w_uv, w_out, ln2_g, ln2_b, ffn2_w1, ffn2_w3, ffn2_w2, ln3_g, ln3_b):
    assert w_in.shape[0] == DEPTH == 1
    B, S, _ = x_prompt.shape
    Bd, Td, _ = x_sample.shape
    n_real, n_dec = B * S, Bd * Td
    l = 0
    vec = lambda a: a[l].reshape(1, -1)
    wts = _pack_weights(w_in[l], b_gates[l], mlstm_norm_g[l], q_norm_g[l], w_uq[l], kv_norm_g[l],
                        w_uk[l], w_uv[l], w_out[l])
    wts.update({"ln2_g": vec(ln2_g), "ln2_b": vec(ln2_b), "ln3_g": vec(ln3_g), "ln3_b": vec(ln3_b),
                "ffn2_w1": ffn2_w1[l].astype(BF16), "ffn2_w3": ffn2_w3[l].astype(BF16),
                "ffn2_w2": ffn2_w2[l].astype(BF16)})
    f1 = (ffn1_w1[l].astype(BF16), ffn1_w3[l].astype(BF16), ffn1_w2[l].astype(BF16), vec(ln1_g), vec(ln1_b))

    x_real = x_prompt.reshape(n_real, D_MODEL)
    x_extra = jnp.concatenate([meta_tokens.astype(F32), x_sample.reshape(n_dec, D_MODEL)], 0)
    n_extra = x_extra.shape[0]
    tm = 512

    h_real = _ffn_ln(x_real, *f1, tm)
    h_extra = _ffn_ln(x_extra, *f1, n_extra)

    pos_e = jnp.concatenate([jnp.arange(N_META),
                             jnp.tile(page_table.shape[1] * PAGE_SIZE + jnp.arange(Td), Bd)])
    (q_r, kt_r, v_r, om_r, g_r, gt_r, aqt_r, ak_r, avt_r, ckv_r, kr_r) = _mixer_inputs(
        h_real, wts, _rope_tables(N_META + jnp.arange(S)), tm, S // tm)
    (q_e, kt_e, v_e, om_e, g_e, gt_e, aqt_e, ak_e, avt_e, ckv_e, kr_e) = _mixer_inputs(
        h_extra, wts, _rope_tables(pos_e), n_extra, 1)

    cs, ms = (1, M_HEADS, HEAD_TILE, 2 * LANES), (1, M_HEADS, 8, LANES)
    _, c_meta, m_meta = _mlstm_scan(q_e[:N_META], kt_e[:, :N_META], v_e[:N_META], g_e[:N_META],
                                    gt_e[:, :N_META], jnp.zeros(cs, F32), jnp.zeros(ms, F32), 1, N_META, N_META)
    hm_r, c_p, m_p = _mlstm_scan(q_r, kt_r, v_r, g_r, gt_r, c_meta, m_meta, B, S, MLSTM_CHUNK)
    mlstm_c_p = c_p[:, :, :M_DQK, :M_DV][None]
    mlstm_n_p = c_p[:, :, :M_DQK, M_DV][None]
    mlstm_m_p = m_p[:, :, 0, 0][None]

    G = Bd * M_HEADS
    def dec_heads(a, w, used):
        return a.reshape(Bd, Td, M_HEADS, w)[..., :used].transpose(0, 2, 1, 3).reshape(G, Td, used)
    q_d = dec_heads(q_e[N_META:], HEAD_TILE, M_DQK)
    v_d = dec_heads(v_e[N_META:], M_DV, M_DV)
    kt_d = (kt_e[:, N_META:].reshape(M_HEADS, HEAD_TILE, Bd, Td)[:, :M_DQK]
            .transpose(2, 0, 1, 3).reshape(G, M_DQK, Td))
    k_d = kt_d.transpose(0, 2, 1)
    gdec = g_e[N_META:].reshape(Bd, Td, 2, M_HEADS).transpose(0, 3, 1, 2).reshape(G, Td, 2)
    hm_d, c_d, n_d, m_d = _mlstm_step(
        q_d, k_d, kt_d, v_d, gdec, gdec.transpose(0, 2, 1),
        state_mlstm_C[l].reshape(G, M_DQK, M_DV), state_mlstm_n[l].reshape(G, 1, M_DQK),
        state_mlstm_m[l].reshape(G, 1, 1), 64)
    hm_d = hm_d.reshape(Bd, M_HEADS, Td, M_DV).transpose(0, 2, 1, 3).reshape(n_dec, M_HEADS * M_DV)

    attn_r = _attention_prompt(aqt_r, ak_r, avt_r, ak_e[:, :N_META], avt_e[:, :, :N_META], B, S)

    aqt_d = aqt_e[:, :, N_META:]
    q_lat_t = _absorb_q(aqt_d, wts["w_uk_heads"])
    rows = A_HEADS * Td
    by_seq = lambda a: (a.reshape(A_HEADS, a.shape[1], Bd, Td).transpose(2, 0, 3, 1)
                        .reshape(Bd, rows, a.shape[1]))
    o_lat = _decode_attention(
        page_table, by_seq(q_lat_t), by_seq(aqt_d[:, :A_ROPE]),
        ckv_e[N_META:].reshape(Bd, Td, KV_LORA), kr_e[N_META:].reshape(Bd, Td, A_ROPE),
        cache_kv_latent[l], jnp.swapaxes(cache_k_rope[l], 1, 2))
    o_lat = o_lat.reshape(Bd, A_HEADS, Td, KV_LORA).transpose(1, 0, 2, 3).reshape(A_HEADS, n_dec, KV_LORA)
    attn_d = _lat2attn(o_lat, wts["w_uv_heads"])

    y_real = _mixer_output_ffn(h_real, hm_r, om_r, attn_r, wts, tm)
    y_dec = _mixer_output_ffn(h_extra[N_META:], hm_d, om_e[N_META:], attn_d, wts, n_dec)

    bc = lambda a: jnp.broadcast_to(a[None], (B,) + a.shape)
    kv_p = jnp.concatenate([bc(ckv_e[:N_META]), ckv_r.reshape(B, S, KV_LORA)], 1)[None]
    kr_p = jnp.concatenate([bc(kr_e[:N_META]), kr_r.reshape(B, S, A_ROPE)], 1)[None]
    return (y_real.reshape(B, S, D_MODEL), y_dec.reshape(Bd, Td, D_MODEL), kv_p, kr_p,
            mlstm_c_p, mlstm_n_p, mlstm_m_p,
            ckv_e[N_META:].reshape(1, Bd, Td, KV_LORA), kr_e[N_META:].reshape(1, Bd, Td, A_ROPE),
            c_d.reshape(1, Bd, M_HEADS, M_DQK, M_DV), n_d.reshape(1, Bd, M_HEADS, M_DQK),
            m_d.reshape(1, Bd, M_HEADS))
```

```python
import functools
import math

import numpy as np
import jax
import jax.numpy as jnp
from jax import lax
from jax.experimental import pallas as pl
from jax.experimental.pallas import tpu as pltpu

F32 = jnp.float32
BF16 = jnp.bfloat16

D_MODEL = 1024
N_META = 16
M_HEADS = 4
M_DQK = 64
M_DV = 128
A_HEADS = 8
A_NOPE = 64
A_ROPE = 32
A_DV = 64
Q_LORA = 256
KV_LORA = 256
ROPE_BASE = 10000.0
PAGE_SIZE = 128
D_FF = 2816
DEPTH = 1
ALPHA = (2.0 * DEPTH) ** 0.25
LN_EPS = 1e-5
RMS_EPS = 1e-6
ATTN_SCALE = (A_NOPE + A_ROPE) ** -0.5
LOG2E = math.log2(math.e)

LANES = 128
HEAD_TILE = LANES
FF_CHUNK = 256
MLSTM_CHUNK = 256
ATTN_BLOCK = 1024
V_ROWS = 80
DEC_SUB = 8
DEC_CHUNK = 32
DEC_SLOTS = 4
VMEM_LIMIT = 56 * 1024 * 1024

_C_Q = 0
_C_V = _C_Q + M_HEADS * HEAD_TILE
_C_O = _C_V + M_HEADS * M_DV
_C_CQ = _C_O + M_HEADS * M_DV
_C_CKV = _C_CQ + Q_LORA
_C_G = _C_CKV + KV_LORA
_C_KRA = _C_G + LANES
_C_KRB = _C_KRA + LANES
_C_END = _C_KRB + LANES


def _cparams(sem):
    return pltpu.CompilerParams(dimension_semantics=sem, vmem_limit_bytes=VMEM_LIMIT)


def _const_spec(shape):
    nd = len(shape)
    return pl.BlockSpec(shape, lambda *_: (0,) * nd, pipeline_mode=pl.Buffered(1))


def _layer_norm(r, g, b):
    mu = jnp.mean(r, axis=-1, keepdims=True)
    d = r - mu
    var = jnp.mean(d * d, axis=-1, keepdims=True)
    return d * lax.rsqrt(var + LN_EPS) * g + b


def _rms_norm(x, g):
    return x * lax.rsqrt(jnp.mean(x * x, axis=-1, keepdims=True) + RMS_EPS) * g


def _dot(a, b):
    return jnp.dot(a, b, preferred_element_type=F32)


def _dot_nt(a, b):
    return lax.dot_general(a, b, (((1,), (1,)), ((), ())), preferred_element_type=F32)


def _swiglu_ln(x, w1_ref, w3_ref, w2_ref, g, b):
    xb = x.astype(BF16)
    y = None
    for j in range(D_FF // FF_CHUNK):
        sl = slice(j * FF_CHUNK, (j + 1) * FF_CHUNK)
        a = _dot(xb, w1_ref[:, sl])
        c = _dot(xb, w3_ref[:, sl])
        hj = (a * jax.nn.sigmoid(a) * c).astype(BF16)
        yj = _dot(hj, w2_ref[sl, :])
        y = yj if y is None else y + yj
    return _layer_norm(ALPHA * x + 0.5 * y, g, b)


def _cast_kernel(*refs):
    n = len(refs) // 2
    for src, dst in zip(refs[:n], refs[n:]):
        dst[...] = src[...].astype(dst.dtype)


def _to_bf16(arrays, steps=8):
    specs = [pl.BlockSpec((a.shape[0] // steps, a.shape[1]), lambda i: (i, 0)) for a in arrays]
    return pl.pallas_call(
        _cast_kernel,
        grid=(steps,),
        in_specs=specs,
        out_specs=specs,
        out_shape=[jax.ShapeDtypeStruct(a.shape, BF16) for a in arrays],
        compiler_params=_cparams(("parallel",)),
        name="cast_bf16",
    )(*arrays)


def _ffn_kernel(x_ref, w1_ref, w3_ref, w2_ref, g_ref, b_ref, o_ref):
    o_ref[...] = _swiglu_ln(x_ref[...], w1_ref, w3_ref, w2_ref, g_ref[...], b_ref[...])


def _ffn_ln(x, w1, w3, w2, g, b, tm):
    n = x.shape[0]
    row = pl.BlockSpec((tm, D_MODEL), lambda i: (i, 0))
    return pl.pallas_call(
        _ffn_kernel,
        grid=(n // tm,),
        in_specs=[row, _const_spec(w1.shape), _const_spec(w3.shape), _const_spec(w2.shape),
                  _const_spec(g.shape), _const_spec(b.shape)],
        out_specs=row,
        out_shape=jax.ShapeDtypeStruct((n, D_MODEL), F32),
        compiler_params=_cparams(("parallel",)),
        name="ffn_ln",
    )(x, w1, w3, w2, g, b)


def _mixin_kernel(x_ref, w_ref, wt_ref, wgt_ref, bg_ref, bgt_ref, qg_ref, wqt_ref, kvg_ref, wuk_ref, wuvt_ref,
                  cos_ref, sin_ref, cost_ref, sint_ref,
                  q_ref, kt_ref, v_ref, o_ref, g_ref, gt_ref, aqt_ref, ak_ref, avt_ref, ckv_ref, kr_ref):
    xb = x_ref[...].astype(BF16)
    z = _dot(xb, w_ref[...])
    q_ref[...] = z[:, _C_Q:_C_V].astype(BF16)
    v_ref[...] = z[:, _C_V:_C_O].astype(BF16)
    o_ref[...] = z[:, _C_O:_C_CQ]
    def gates(pre, is_forget):
        logsig = jnp.minimum(pre, 0.0) - jnp.log1p(jnp.exp(-jnp.abs(pre)))
        return jnp.where(is_forget, logsig, pre)
    gc = z[:, _C_G:_C_G + 2 * M_HEADS] + bg_ref[...]
    g_ref[...] = gates(gc, lax.broadcasted_iota(jnp.int32, gc.shape, 1) >= M_HEADS)
    gr = _dot_nt(wgt_ref[...], xb)[:2 * M_HEADS] + bgt_ref[...]
    gt_ref[...] = gates(gr, lax.broadcasted_iota(jnp.int32, gr.shape, 0) >= M_HEADS)
    kt_ref[...] = _dot_nt(wt_ref[...], xb).astype(BF16)
    cq = _rms_norm(z[:, _C_CQ:_C_CKV], qg_ref[...]).astype(BF16)
    qat = _dot_nt(wqt_ref[...], cq)
    ckv = _rms_norm(z[:, _C_CKV:_C_G], kvg_ref[...])
    ckv_ref[...] = ckv
    ckv_b = ckv.astype(BF16)
    kn = _dot(ckv_b, wuk_ref[...])
    vt = _dot_nt(wuvt_ref[...], ckv_b)
    kr = z[:, _C_KRA:_C_KRB] * cos_ref[...] + z[:, _C_KRB:_C_END] * sin_ref[...]
    kr_ref[...] = kr[:, :A_ROPE]
    cost = cost_ref[...]
    sint = sint_ref[...]
    ones_row = (lax.broadcasted_iota(jnp.int32, (V_ROWS, 1), 0) == A_DV).astype(F32)
    nh = A_HEADS * HEAD_TILE
    for h in range(A_HEADS):
        sl = slice(h * HEAD_TILE, (h + 1) * HEAD_TILE)
        sw = slice(nh + h * HEAD_TILE, nh + (h + 1) * HEAD_TILE)
        aqt_ref[h] = ((qat[sl] * cost + qat[sw] * sint) * (ATTN_SCALE * LOG2E)).astype(BF16)
        ak_ref[h] = (kn[:, sl] + kr).astype(BF16)
        avt_ref[h] = (vt[h * V_ROWS:(h + 1) * V_ROWS] + ones_row).astype(BF16)


def _mixer_inputs(x, wts, tables, tm, n_tab_blocks):
    n = x.shape[0]
    cos, sin = tables
    row = lambda w: pl.BlockSpec((tm, w), lambda i: (i, 0))
    colb = lambda r: pl.BlockSpec((r, tm), lambda i: (0, i))
    head = pl.BlockSpec((A_HEADS, tm, HEAD_TILE), lambda i: (0, i, 0))
    head_t = lambda r: pl.BlockSpec((A_HEADS, r, tm), lambda i: (0, 0, i))
    tab = pl.BlockSpec((tm, HEAD_TILE), lambda i: (i % n_tab_blocks, 0))
    tab_t = pl.BlockSpec((HEAD_TILE, tm), lambda i: (0, i % n_tab_blocks))
    consts = [wts["w_main"], wts["w_kt"], wts["w_gt"], wts["bg_row"], wts["bg_col"], wts["q_norm_g"],
              wts["w_q2t"], wts["kv_norm_g"], wts["w_uk_tiles"], wts["w_uvt"]]
    hq = M_HEADS * HEAD_TILE
    hv = M_HEADS * M_DV
    out_shape = (
        jax.ShapeDtypeStruct((n, hq), BF16),
        jax.ShapeDtypeStruct((hq, n), BF16),
        jax.ShapeDtypeStruct((n, hv), BF16),
        jax.ShapeDtypeStruct((n, hv), F32),
        jax.ShapeDtypeStruct((n, 2 * M_HEADS), F32),
        jax.ShapeDtypeStruct((2 * M_HEADS, n), F32),
        jax.ShapeDtypeStruct((A_HEADS, HEAD_TILE, n), BF16),
        jax.ShapeDtypeStruct((A_HEADS, n, HEAD_TILE), BF16),
        jax.ShapeDtypeStruct((A_HEADS, V_ROWS, n), BF16),
        jax.ShapeDtypeStruct((n, KV_LORA), F32),
        jax.ShapeDtypeStruct((n, A_ROPE), F32),
    )
    out_specs = (row(hq), colb(hq), row(hv), row(hv), row(2 * M_HEADS), colb(2 * M_HEADS),
                 head_t(HEAD_TILE), head, head_t(V_ROWS), row(KV_LORA), row(A_ROPE))
    return pl.pallas_call(
        _mixin_kernel,
        grid=(n // tm,),
        in_specs=[row(D_MODEL)] + [_const_spec(c.shape) for c in consts] + [tab, tab, tab_t, tab_t],
        out_specs=out_specs,
        out_shape=out_shape,
        compiler_params=_cparams(("parallel",)),
        name="mixer_inputs",
    )(x, *consts, cos, sin, cos.T, sin.T)


def _split3(x):
    hi = x.astype(BF16)
    r1 = x - hi.astype(F32)
    mid = r1.astype(BF16)
    lo = (r1 - mid.astype(F32)).astype(BF16)
    return hi, mid, lo


def _mlstm_kernel(q_ref, kt_ref, v_ref, g_ref, gt_ref, c0_ref, m0_ref, h_ref, c_ref, m_ref, c_sc, m_sc,
                  *, chunk):
    ci = pl.program_id(1)

    @pl.when(ci == 0)
    def _():
        c_sc[...] = c0_ref[0]
        m_sc[...] = m0_ref[0]

    L = chunk
    row = lax.broadcasted_iota(jnp.int32, (L, L), 0)
    col = lax.broadcasted_iota(jnp.int32, (L, L), 1)
    causal = row >= col
    tril = causal.astype(BF16)
    triu = (row <= col).astype(BF16)
    g = g_ref[...]
    gt = gt_ref[...]
    cum_c = sum(_dot(tril, p) for p in _split3(g))
    cum_r = sum(_dot(p, triu) for p in _split3(gt))
    ones_col = (lax.broadcasted_iota(jnp.int32, (L, LANES), 1) == 0).astype(BF16)
    for h in range(M_HEADS):
        sl = slice(h * HEAD_TILE, (h + 1) * HEAD_TILE)
        qh = q_ref[:, sl]
        kth = kt_ref[sl, :]
        v_aug = jnp.concatenate([v_ref[:, h * M_DV:(h + 1) * M_DV], ones_col], axis=1)
        ig_r = gt[h:h + 1, :]
        lf_r = gt[M_HEADS + h:M_HEADS + h + 1, :]
        cr = cum_r[M_HEADS + h:M_HEADS + h + 1, :]
        cc = cum_c[:, M_HEADS + h:M_HEADS + h + 1]
        m_run = m_sc[h, 0:1, 0:1]
        d = jnp.where(causal, cc - cr + ig_r, -jnp.inf)
        inter = cc + m_run
        m_t = jnp.maximum(inter, jnp.max(d, axis=1, keepdims=True))
        s = (_dot(qh, kth) * jnp.exp(d - m_t)).astype(BF16)
        w_inter = jnp.exp(inter - m_t)
        c_state = c_sc[h]
        r = _dot(s, v_aug) + w_inter * _dot(qh, c_state.astype(BF16))
        qn = r[:, M_DV:M_DV + 1]
        den = jnp.maximum(jnp.abs(qn), jnp.exp(-m_t))
        h_ref[:, h * M_DV:(h + 1) * M_DV] = r[:, :M_DV] / den
        last = jnp.sum(lf_r, axis=1, keepdims=True)
        g_r = last - cr + ig_r
        m_new = jnp.maximum(last + m_run, jnp.max(g_r, axis=1, keepdims=True))
        decay = jnp.exp(last + m_run - m_new)
        wkt = (kth.astype(F32) * jnp.exp(g_r - m_new)).astype(BF16)
        c_sc[h] = decay * c_state + _dot(wkt, v_aug)
        m_sc[h] = jnp.broadcast_to(m_new, (8, LANES))

    @pl.when(ci == pl.num_programs(1) - 1)
    def _():
        c_ref[0] = c_sc[...]
        m_ref[0] = m_sc[...]


def _mlstm_scan(q, kt, v, g, gt, c0, m0, batch, seq, chunk):
    nc = seq // chunk
    hq = M_HEADS * HEAD_TILE
    hv = M_HEADS * M_DV
    row = lambda w: pl.BlockSpec((chunk, w), lambda b, c: (b * nc + c, 0))
    colb = lambda r: pl.BlockSpec((r, chunk), lambda b, c: (0, b * nc + c))
    st_in = lambda s: pl.BlockSpec((1,) + s, lambda b, c: (0,) * (len(s) + 1))
    st_out = lambda s: pl.BlockSpec((1,) + s, lambda b, c: (b,) + (0,) * len(s))
    cs, ms = (M_HEADS, HEAD_TILE, 2 * LANES), (M_HEADS, 8, LANES)
    return pl.pallas_call(
        functools.partial(_mlstm_kernel, chunk=chunk),
        grid=(batch, nc),
        in_specs=[row(hq), colb(hq), row(hv), row(2 * M_HEADS), colb(2 * M_HEADS), st_in(cs), st_in(ms)],
        out_specs=(row(hv), st_out(cs), st_out(ms)),
        out_shape=(jax.ShapeDtypeStruct((batch * seq, hv), F32),
                   jax.ShapeDtypeStruct((batch,) + cs, F32),
                   jax.ShapeDtypeStruct((batch,) + ms, F32)),
        scratch_shapes=[pltpu.VMEM(cs, F32), pltpu.VMEM(ms, F32)],
        compiler_params=_cparams(("parallel", "arbitrary")),
        name="mlstm_scan",
    )(q, kt, v, g, gt, c0, m0)


def _mlstm_step_kernel(q_ref, k_ref, kt_ref, v_ref, gc_ref, gr_ref, c0_ref, n0_ref, m0_ref,
                       h_ref, c_ref, n_ref, m_ref):
    T = q_ref.shape[1]
    row = lax.broadcasted_iota(jnp.int32, (T, T), 0)
    col = lax.broadcasted_iota(jnp.int32, (T, T), 1)
    causal = (row >= col)[None]
    q = q_ref[...]
    k = k_ref[...]
    v = v_ref[...]
    ig_c, lf_c = gc_ref[:, :, 0:1], gc_ref[:, :, 1:2]
    ig_r, lf_r = gr_ref[:, 0:1, :], gr_ref[:, 1:2, :]
    m0 = m0_ref[...]
    c0 = c0_ref[...]
    n0 = n0_ref[...]
    cum_c = jnp.sum(jnp.where(causal, lf_r, 0.0), axis=2, keepdims=True)
    cum_r = jnp.sum(jnp.where((row <= col)[None], lf_c, 0.0), axis=1, keepdims=True)
    d = jnp.where(causal, cum_c - cum_r + ig_r, -jnp.inf)
    inter = cum_c + m0
    m_t = jnp.maximum(inter, jnp.max(d, axis=2, keepdims=True))
    qk = jnp.einsum("gtd,gsd->gts", q, k, preferred_element_type=F32)
    s = qk * jnp.exp(d - m_t)
    w_inter = jnp.exp(inter - m_t)
    num = (jnp.einsum("gts,gsv->gtv", s.astype(BF16), v, preferred_element_type=F32)
           + w_inter * jnp.einsum("gtd,gdv->gtv", q, c0.astype(BF16), preferred_element_type=F32))
    qn = jnp.sum(s, axis=2, keepdims=True) + w_inter * jnp.sum(q.astype(F32) * n0, axis=2, keepdims=True)
    h_ref[...] = num / jnp.maximum(jnp.abs(qn), jnp.exp(-m_t))
    last = cum_c[:, T - 1:T, :]
    g_c = last - cum_c + ig_c
    g_r = last - cum_r + ig_r
    m_new = jnp.maximum(last + m0, jnp.max(g_c, axis=1, keepdims=True))
    decay = jnp.exp(last + m0 - m_new)
    wk = k.astype(F32) * jnp.exp(g_c - m_new)
    wkt = (kt_ref[...].astype(F32) * jnp.exp(g_r - m_new)).astype(BF16)
    c_ref[...] = decay * c0 + jnp.einsum("gds,gsv->gdv", wkt, v, preferred_element_type=F32)
    n_ref[...] = decay * n0 + jnp.sum(wk, axis=1, keepdims=True)
    m_ref[...] = m_new


def _mlstm_step(q, k, kt, v, gc, gr, c0, n0, m0, gb):
    gtot, T = q.shape[0], q.shape[1]
    blk = lambda a, b: pl.BlockSpec((gb, a, b), lambda i: (i, 0, 0))
    return pl.pallas_call(
        _mlstm_step_kernel,
        grid=(gtot // gb,),
        in_specs=[blk(T, M_DQK), blk(T, M_DQK), blk(M_DQK, T), blk(T, M_DV), blk(T, 2), blk(2, T),
                  blk(M_DQK, M_DV), blk(1, M_DQK), blk(1, 1)],
        out_specs=(blk(T, M_DV), blk(M_DQK, M_DV), blk(1, M_DQK), blk(1, 1)),
        out_shape=(jax.ShapeDtypeStruct((gtot, T, M_DV), F32),
                   jax.ShapeDtypeStruct((gtot, M_DQK, M_DV), F32),
                   jax.ShapeDtypeStruct((gtot, 1, M_DQK), F32),
                   jax.ShapeDtypeStruct((gtot, 1, 1), F32)),
        compiler_params=_cparams(("parallel",)),
        name="mlstm_step",
    )(q, k, kt, v, gc, gr, c0, n0, m0)


def _attn_kernel(qi_tab, ki_tab, qt_ref, k_ref, vt_ref, km_ref, vmt_ref, o_ref, acc_sc, m_sc):
    step = pl.program_id(1)
    qi = qi_tab[step]
    ki = ki_tab[step]
    tq = qt_ref.shape[2]

    @pl.when(ki == 0)
    def _():
        ss = [_dot(km_ref[h], qt_ref[h]) for h in range(A_HEADS)]
        ms = [jnp.max(s, axis=0, keepdims=True) for s in ss]
        for h in range(A_HEADS):
            acc_sc[h] = _dot(vmt_ref[h], jnp.exp2(ss[h] - ms[h]).astype(BF16))
            m_sc[h] = ms[h]

    def update(masked):
        if masked:
            row = lax.broadcasted_iota(jnp.int32, (tq, tq), 0)
            col = lax.broadcasted_iota(jnp.int32, (tq, tq), 1)
            keep = row <= col
        s_next = _dot(k_ref[0], qt_ref[0])
        for h in range(A_HEADS):
            s = s_next
            if h + 1 < A_HEADS:
                s_next = _dot(k_ref[h + 1], qt_ref[h + 1])
            if masked:
                s = jnp.where(keep, s, -jnp.inf)
            m_prev = m_sc[h]
            m_new = jnp.maximum(m_prev, jnp.max(s, axis=0, keepdims=True))
            p = jnp.exp2(s - m_new).astype(BF16)
            acc_sc[h] = jnp.exp2(m_prev - m_new) * acc_sc[h] + _dot(vt_ref[h], p)
            m_sc[h] = m_new

    @pl.when(ki < qi)
    def _():
        update(False)

    @pl.when(ki == qi)
    def _():
        update(True)
        for j in range(A_HEADS // 2):
            ae = acc_sc[2 * j]
            ao = acc_sc[2 * j + 1]
            pair = jnp.concatenate([ae[:A_DV] / ae[A_DV:A_DV + 1], ao[:A_DV] / ao[A_DV:A_DV + 1]], axis=0)
            o_ref[:, j * HEAD_TILE:(j + 1) * HEAD_TILE] = pair.T.astype(o_ref.dtype)


def _attention_prompt(aqt, ak, avt, k_meta, vt_meta, batch, seq):
    blk = ATTN_BLOCK
    nq = seq // blk
    qi_np, ki_np = zip(*[(i, j) for i in range(nq) for j in range(i + 1)])
    qi_tab = jnp.asarray(np.array(qi_np, np.int32))
    ki_tab = jnp.asarray(np.array(ki_np, np.int32))
    rows = lambda tab: pl.BlockSpec((A_HEADS, blk, HEAD_TILE),
                                    lambda b, s, qt, kt: (0, b * nq + (qt, kt)[tab][s], 0))
    cols = lambda tab, r: pl.BlockSpec((A_HEADS, r, blk),
                                       lambda b, s, qt, kt: (0, 0, b * nq + (qt, kt)[tab][s]))
    grid_spec = pltpu.PrefetchScalarGridSpec(
        num_scalar_prefetch=2,
        grid=(batch, len(qi_np)),
        in_specs=[cols(0, HEAD_TILE), rows(1), cols(1, V_ROWS),
                  pl.BlockSpec((A_HEADS, N_META, HEAD_TILE), lambda b, s, qt, kt: (0, 0, 0)),
                  pl.BlockSpec((A_HEADS, V_ROWS, N_META), lambda b, s, qt, kt: (0, 0, 0))],
        out_specs=pl.BlockSpec((blk, A_HEADS * A_DV), lambda b, s, qt, kt: (b * nq + qt[s], 0)),
        scratch_shapes=[pltpu.VMEM((A_HEADS, V_ROWS, blk), F32), pltpu.VMEM((A_HEADS, 1, blk), F32)],
    )
    return pl.pallas_call(
        _attn_kernel,
        grid_spec=grid_spec,
        out_shape=jax.ShapeDtypeStruct((batch * seq, A_HEADS * A_DV), BF16),
        compiler_params=_cparams(("parallel", "arbitrary")),
        name="attn_prompt",
    )(qi_tab, ki_tab, aqt, ak, avt, k_meta, vt_meta)


def _absorb_kernel(qt_ref, wuk_ref, o_ref):
    for h in range(A_HEADS):
        o_ref[h] = _dot(wuk_ref[h], qt_ref[h]).astype(BF16)


def _absorb_q(aqt, wuk_tiles):
    n = aqt.shape[2]
    return pl.pallas_call(
        _absorb_kernel,
        out_shape=jax.ShapeDtypeStruct((A_HEADS, KV_LORA, n), BF16),
        compiler_params=pltpu.CompilerParams(vmem_limit_bytes=VMEM_LIMIT),
        name="absorb_q",
    )(aqt, wuk_tiles)


def _decode_attn_kernel(pt_ref, ql_ref, qr_ref, cn_ref, kn_ref, lat_hbm, krt_hbm, o_ref,
                        lat_buf, krt_buf, sem_lat, sem_krt):
    b = pl.program_id(0)
    nb = pl.num_programs(0)
    sub, chunk, n_slots = DEC_SUB, DEC_CHUNK, DEC_SLOTS
    n_sub = pt_ref.shape[1] // sub
    spc = chunk // sub
    n_chunks = n_sub // spc
    ahead = 2 * spc
    assert n_chunks % n_slots == 0 and n_slots >= 4

    def copies(bb, j):
        slot = (j // spc) % n_slots
        for k in range((j % spc) * sub, (j % spc + 1) * sub):
            page = pt_ref[bb, (j // spc) * chunk + k]
            yield pltpu.make_async_copy(lat_hbm.at[page], lat_buf.at[slot, k], sem_lat.at[slot])
            yield pltpu.make_async_copy(krt_hbm.at[page], krt_buf.at[slot, k], sem_krt.at[slot])

    def start(bb, j):
        for cp in copies(bb, j):
            cp.start()

    def wait_chunk(c):
        for j in range(c * spc, (c + 1) * spc):
            for cp in copies(b, j):
                cp.wait()

    @pl.when(b == 0)
    def _():
        for j in range(ahead):
            start(0, j)

    ql = ql_ref[0]
    qr = qr_ref[0]
    rows = ql.shape[0]
    m_run = jnp.full((rows, 1), -jnp.inf, F32)
    l_run = jnp.zeros((rows, 1), F32)
    acc = jnp.zeros((rows, KV_LORA), F32)

    def scores(j):
        slot = (j // spc) % n_slots
        ks = range((j % spc) * sub, (j % spc + 1) * sub)
        return jnp.concatenate(
            [_dot_nt(ql, lat_buf[slot, k].astype(BF16)) + _dot(qr, krt_buf[slot, k].astype(BF16)) for k in ks],
            axis=1)

    def weighted_values(j, pb):
        slot = (j // spc) % n_slots
        ks = range((j % spc) * sub, (j % spc + 1) * sub)
        return sum(_dot(pb[:, n * PAGE_SIZE:(n + 1) * PAGE_SIZE], lat_buf[slot, k].astype(BF16))
                   for n, k in enumerate(ks))

    wait_chunk(0)
    s_next = scores(0)
    pending = None
    for i in range(n_sub):
        s = s_next
        if i + 1 < n_sub:
            if (i + 1) % spc == 0:
                wait_chunk((i + 1) // spc)
            s_next = scores(i + 1)
        j = i + ahead
        if j < n_sub:
            start(b, j)
        else:
            @pl.when(b + 1 < nb)
            def _():
                start(b + 1, j - n_sub)
        if pending is not None:
            jp, pb, alpha = pending
            acc = alpha * acc + weighted_values(jp, pb)
        m_new = jnp.maximum(m_run, jnp.max(s, axis=1, keepdims=True))
        p = jnp.exp2(s - m_new)
        alpha = jnp.exp2(m_run - m_new)
        l_run = alpha * l_run + jnp.sum(p, axis=1, keepdims=True)
        pending = (i, p.astype(BF16), alpha)
        m_run = m_new
    jp, pb, alpha = pending
    acc = alpha * acc + weighted_values(jp, pb)

    T = cn_ref.shape[1]
    cn = cn_ref[0].astype(BF16).astype(F32)
    kn = kn_ref[0].astype(BF16).astype(F32)
    qlf = ql.astype(F32)
    qrf = qr.astype(F32)
    t_of_row = lax.broadcasted_iota(jnp.int32, (rows, 1), 0) % T
    s_new = []
    for t in range(T):
        st = (jnp.sum(qlf * cn[t:t + 1], axis=1, keepdims=True)
              + jnp.sum(qrf * kn[t:t + 1], axis=1, keepdims=True))
        s_new.append(jnp.where(t_of_row >= t, st, -jnp.inf))
    m_fin = m_run
    for st in s_new:
        m_fin = jnp.maximum(m_fin, st)
    alpha = jnp.exp2(m_run - m_fin)
    acc = alpha * acc
    l_run = alpha * l_run
    for t, st in enumerate(s_new):
        pt = jnp.exp2(st - m_fin)
        l_run = l_run + pt
        acc = acc + pt * cn[t:t + 1]
    o_ref[0] = acc / l_run


def _decode_attention(page_table, q_lat, q_rope, c_new, kr_new, cache_lat, cache_krt):
    nb = page_table.shape[0]
    rows = q_lat.shape[1]
    per_b = lambda a, b: pl.BlockSpec((1, a, b), lambda i, pt: (i, 0, 0))
    hbm = pl.BlockSpec(memory_space=pl.ANY)
    grid_spec = pltpu.PrefetchScalarGridSpec(
        num_scalar_prefetch=1,
        grid=(nb,),
        in_specs=[per_b(rows, KV_LORA), per_b(rows, A_ROPE), per_b(c_new.shape[1], KV_LORA),
                  per_b(kr_new.shape[1], A_ROPE), hbm, hbm],
        out_specs=per_b(rows, KV_LORA),
        scratch_shapes=[pltpu.VMEM((DEC_SLOTS, DEC_CHUNK, PAGE_SIZE, KV_LORA), F32),
                        pltpu.VMEM((DEC_SLOTS, DEC_CHUNK, A_ROPE, PAGE_SIZE), F32),
                        pltpu.SemaphoreType.DMA((DEC_SLOTS,)), pltpu.SemaphoreType.DMA((DEC_SLOTS,))],
    )
    return pl.pallas_call(
        _decode_attn_kernel,
        grid_spec=grid_spec,
        out_shape=jax.ShapeDtypeStruct((nb, rows, KV_LORA), F32),
        compiler_params=_cparams(("arbitrary",)),
        name="attn_decode",
    )(page_table, q_lat, q_rope, c_new, kr_new, cache_lat, cache_krt)


def _lat2attn_kernel(o_ref, wuv_ref, a_ref):
    for j in range(A_HEADS // 2):
        pair = (_dot(o_ref[2 * j].astype(BF16), wuv_ref[2 * j])
                + _dot(o_ref[2 * j + 1].astype(BF16), wuv_ref[2 * j + 1]))
        a_ref[:, j * HEAD_TILE:(j + 1) * HEAD_TILE] = pair.astype(a_ref.dtype)


def _lat2attn(o_lat, wuv_tiles):
    n = o_lat.shape[1]
    return pl.pallas_call(
        _lat2attn_kernel,
        out_shape=jax.ShapeDtypeStruct((n, A_HEADS * A_DV), BF16),
        compiler_params=pltpu.CompilerParams(vmem_limit_bytes=VMEM_LIMIT),
        name="lat2attn",
    )(o_lat, wuv_tiles)


def _mixout_kernel(x_ref, hm_ref, om_ref, at_ref, ng_ref, wo_ref, g2_ref, b2_ref,
                   w1_ref, w3_ref, w2_ref, g3_ref, b3_ref, o_ref):
    x = x_ref[...]
    parts = []
    for h in range(M_HEADS):
        sl = slice(h * M_DV, (h + 1) * M_DV)
        hn = _rms_norm(hm_ref[:, sl], ng_ref[:, sl])
        parts.append((hn * jax.nn.sigmoid(om_ref[:, sl])).astype(BF16))
    cat = jnp.concatenate(parts + [at_ref[...]], axis=1)
    x2 = _layer_norm(ALPHA * x + _dot(cat, wo_ref[...]), g2_ref[...], b2_ref[...])
    o_ref[...] = _swiglu_ln(x2, w1_ref, w3_ref, w2_ref, g3_ref[...], b3_ref[...])


def _mixer_output_ffn(x, hm, om, attn, wts, tm):
    n = x.shape[0]
    row = lambda w: pl.BlockSpec((tm, w), lambda i: (i, 0))
    consts = [wts["mlstm_norm_g"], wts["w_out"], wts["ln2_g"], wts["ln2_b"],
              wts["ffn2_w1"], wts["ffn2_w3"], wts["ffn2_w2"], wts["ln3_g"], wts["ln3_b"]]
    hv = M_HEADS * M_DV
    return pl.pallas_call(
        _mixout_kernel,
        grid=(n // tm,),
        in_specs=[row(D_MODEL), row(hv), row(hv), row(A_HEADS * A_DV)] + [_const_spec(c.shape) for c in consts],
        out_specs=row(D_MODEL),
        out_shape=jax.ShapeDtypeStruct((n, D_MODEL), F32),
        compiler_params=_cparams(("parallel",)),
        name="mixer_output_ffn",
    )(x, hm, om, attn, *consts)


def _pack_weights(w_in, b_gates, mlstm_norm_g, q_norm_g, w_uq, kv_norm_g, w_uk, w_uv, w_out):
    o_q, o_k, o_v, o_o = 0, M_HEADS * M_DQK, 2 * M_HEADS * M_DQK, 2 * M_HEADS * M_DQK + M_HEADS * M_DV
    o_g = o_o + M_HEADS * M_DV
    o_cq = o_g + 2 * M_HEADS
    o_ckv = o_cq + Q_LORA
    o_kr = o_ckv + KV_LORA
    half = A_ROPE // 2
    dq = A_NOPE + A_ROPE

    def pad_axis(a, axis, lo, size):
        pads = [(0, 0)] * a.ndim
        pads[axis] = (lo, size - lo - a.shape[axis])
        return jnp.pad(a, pads)

    q_cols = pad_axis(w_in[:, o_q:o_k].reshape(D_MODEL, M_HEADS, M_DQK), 2, 0, HEAD_TILE).reshape(D_MODEL, -1)
    kr = w_in[:, o_kr:o_kr + A_ROPE]
    kr_sw = jnp.concatenate([kr[:, half:], kr[:, :half]], 1)
    w_main = jnp.concatenate(
        [q_cols, w_in[:, o_v:o_g], w_in[:, o_cq:o_kr], pad_axis(w_in[:, o_g:o_cq], 1, 0, LANES),
         pad_axis(kr, 1, 0, LANES), pad_axis(kr_sw, 1, 0, LANES)], 1)
    assert w_main.shape[1] == _C_END
    w_kt = pad_axis((w_in[:, o_k:o_v].T * (M_DQK ** -0.5)).reshape(M_HEADS, M_DQK, D_MODEL), 1, 0, HEAD_TILE)
    w_kt = w_kt.reshape(M_HEADS * HEAD_TILE, D_MODEL)
    w_gt = pad_axis(w_in[:, o_g:o_cq].T, 0, 0, 16)
    uq = w_uq.reshape(Q_LORA, A_HEADS, dq)
    q_tiles = pad_axis(jnp.concatenate([uq[..., A_NOPE:], uq[..., :A_NOPE]], -1), 2, 0, HEAD_TILE)
    q_sw_tiles = pad_axis(jnp.concatenate([uq[..., A_NOPE + half:], uq[..., A_NOPE:A_NOPE + half]], -1),
                          2, 0, HEAD_TILE)
    w_q2 = jnp.concatenate([q_tiles.reshape(Q_LORA, -1), q_sw_tiles.reshape(Q_LORA, -1)], 1)
    uk_tiles = pad_axis(w_uk, 2, A_ROPE, HEAD_TILE)
    w_uvt = pad_axis(w_uv.transpose(1, 2, 0), 1, 0, V_ROWS).reshape(A_HEADS * V_ROWS, KV_LORA)
    uv_pairs = w_uv.reshape(KV_LORA, A_HEADS // 2, 2, A_DV)
    uv_heads = jnp.stack([pad_axis(uv_pairs[:, :, 0], 2, 0, HEAD_TILE),
                          pad_axis(uv_pairs[:, :, 1], 2, A_DV, HEAD_TILE)], 2)
    bf = lambda a: a.astype(BF16)
    return {
        "w_main": bf(w_main), "w_kt": bf(w_kt), "w_gt": bf(w_gt),
        "bg_row": b_gates.reshape(1, -1), "bg_col": b_gates.reshape(-1, 1),
        "q_norm_g": q_norm_g.reshape(1, -1), "kv_norm_g": kv_norm_g.reshape(1, -1),
        "w_q2t": bf(w_q2.T),
        "w_uk_tiles": bf(uk_tiles.reshape(KV_LORA, -1)),
        "w_uvt": bf(w_uvt),
        "w_uk_heads": bf(uk_tiles.transpose(1, 0, 2)),
        "w_uv_heads": bf(uv_heads.reshape(KV_LORA, A_HEADS, HEAD_TILE).transpose(1, 0, 2)),
        "mlstm_norm_g": mlstm_norm_g.reshape(1, -1), "w_out": bf(w_out),
    }


def _rope_tables(pos):
    half = A_ROPE // 2
    inv = ROPE_BASE ** (-np.arange(half, dtype=np.float64) / half)
    ang = np.asarray(pos, np.float64)[:, None] * inv[None, :]
    cos, sin = np.cos(ang), np.sin(ang)
    n = ang.shape[0]
    cos_t = np.concatenate([cos, cos, np.ones((n, A_NOPE)), np.zeros((n, HEAD_TILE - A_NOPE - A_ROPE))], 1)
    sin_t = np.concatenate([-sin, sin, np.zeros((n, HEAD_TILE - A_ROPE))], 1)
    return cos_t.astype(np.float32), sin_t.astype(np.float32)


def kernel(x_prompt, x_sample, cache_kv_latent, cache_k_rope, state_mlstm_C, state_mlstm_n, state_mlstm_m, page_table, meta_tokens, ffn1_w1, ffn1_w3, ffn1_w2, ln1_g, ln1_b, w_in, b_gates, mlstm_norm_g, q_norm_g, w_uq, kv_norm_g, w_uk, w_uv, w_out, ln2_g, ln2_b, ffn2_w1, ffn2_w3, ffn2_w2, ln3_g, ln3_b):
    assert w_in.shape[0] == DEPTH == 1
    B, S, _ = x_prompt.shape
    Bd, Td, _ = x_sample.shape
    n_real, n_dec = B * S, Bd * Td
    l = 0
    vec = lambda a: a[l].reshape(1, -1)
    wts = _pack_weights(w_in[l], b_gates[l], mlstm_norm_g[l], q_norm_g[l], w_uq[l], kv_norm_g[l],
                        w_uk[l], w_uv[l], w_out[l])
    ffn_bf = _to_bf16([ffn1_w1[l], ffn1_w3[l], ffn1_w2[l], ffn2_w1[l], ffn2_w3[l], ffn2_w2[l]])
    wts.update({"ln2_g": vec(ln2_g), "ln2_b": vec(ln2_b), "ln3_g": vec(ln3_g), "ln3_b": vec(ln3_b),
                "ffn2_w1": ffn_bf[3], "ffn2_w3": ffn_bf[4], "ffn2_w2": ffn_bf[5]})
    f1 = (ffn_bf[0], ffn_bf[1], ffn_bf[2], vec(ln1_g), vec(ln1_b))

    x_real = x_prompt.reshape(n_real, D_MODEL)
    x_extra = jnp.concatenate([meta_tokens.astype(F32), x_sample.reshape(n_dec, D_MODEL)], 0)
    n_extra = x_extra.shape[0]
    tm = 512

    h_real = _ffn_ln(x_real, *f1, tm)
    h_extra = _ffn_ln(x_extra, *f1, n_extra)

    pos_e = np.concatenate([np.arange(N_META), np.tile(page_table.shape[1] * PAGE_SIZE + np.arange(Td), Bd)])
    (q_r, kt_r, v_r, om_r, g_r, gt_r, aqt_r, ak_r, avt_r, ckv_r, kr_r) = _mixer_inputs(
        h_real, wts, _rope_tables(N_META + np.arange(S)), tm, S // tm)
    (q_e, kt_e, v_e, om_e, g_e, gt_e, aqt_e, ak_e, avt_e, ckv_e, kr_e) = _mixer_inputs(
        h_extra, wts, _rope_tables(pos_e), n_extra, 1)

    cs, ms = (1, M_HEADS, HEAD_TILE, 2 * LANES), (1, M_HEADS, 8, LANES)
    _, c_meta, m_meta = _mlstm_scan(q_e[:N_META], kt_e[:, :N_META], v_e[:N_META], g_e[:N_META],
                                    gt_e[:, :N_META], jnp.zeros(cs, F32), jnp.zeros(ms, F32), 1, N_META, N_META)
    hm_r, c_p, m_p = _mlstm_scan(q_r, kt_r, v_r, g_r, gt_r, c_meta, m_meta, B, S, MLSTM_CHUNK)
    mlstm_c_p = c_p[:, :, :M_DQK, :M_DV][None]
    mlstm_n_p = c_p[:, :, :M_DQK, M_DV][None]
    mlstm_m_p = m_p[:, :, 0, 0][None]

    G = Bd * M_HEADS
    def dec_heads(a, w, used):
        return a.reshape(Bd, Td, M_HEADS, w)[..., :used].transpose(0, 2, 1, 3).reshape(G, Td, used)
    q_d = dec_heads(q_e[N_META:], HEAD_TILE, M_DQK)
    v_d = dec_heads(v_e[N_META:], M_DV, M_DV)
    kt_d = (kt_e[:, N_META:].reshape(M_HEADS, HEAD_TILE, Bd, Td)[:, :M_DQK]
            .transpose(2, 0, 1, 3).reshape(G, M_DQK, Td))
    k_d = kt_d.transpose(0, 2, 1)
    gdec = g_e[N_META:].reshape(Bd, Td, 2, M_HEADS).transpose(0, 3, 1, 2).reshape(G, Td, 2)
    hm_d, c_d, n_d, m_d = _mlstm_step(
        q_d, k_d, kt_d, v_d, gdec, gdec.transpose(0, 2, 1),
        state_mlstm_C[l].reshape(G, M_DQK, M_DV), state_mlstm_n[l].reshape(G, 1, M_DQK),
        state_mlstm_m[l].reshape(G, 1, 1), 64)
    hm_d = hm_d.reshape(Bd, M_HEADS, Td, M_DV).transpose(0, 2, 1, 3).reshape(n_dec, M_HEADS * M_DV)

    attn_r = _attention_prompt(aqt_r, ak_r, avt_r, ak_e[:, :N_META], avt_e[:, :, :N_META], B, S)

    aqt_d = aqt_e[:, :, N_META:]
    q_lat_t = _absorb_q(aqt_d, wts["w_uk_heads"])
    rows = A_HEADS * Td
    by_seq = lambda a: (a.reshape(A_HEADS, a.shape[1], Bd, Td).transpose(2, 0, 3, 1)
                        .reshape(Bd, rows, a.shape[1]))
    o_lat = _decode_attention(
        page_table, by_seq(q_lat_t), by_seq(aqt_d[:, :A_ROPE]),
        ckv_e[N_META:].reshape(Bd, Td, KV_LORA), kr_e[N_META:].reshape(Bd, Td, A_ROPE),
        cache_kv_latent[l], jnp.swapaxes(cache_k_rope[l], 1, 2))
    o_lat = o_lat.reshape(Bd, A_HEADS, Td, KV_LORA).transpose(1, 0, 2, 3).reshape(A_HEADS, n_dec, KV_LORA)
    attn_d = _lat2attn(o_lat, wts["w_uv_heads"])

    y_real = _mixer_output_ffn(h_real, hm_r, om_r, attn_r, wts, tm)
    y_dec = _mixer_output_ffn(h_extra[N_META:], hm_d, om_e[N_META:], attn_d, wts, n_dec)

    bc = lambda a: jnp.broadcast_to(a[None], (B,) + a.shape)
    kv_p = jnp.concatenate([bc(ckv_e[:N_META]), ckv_r.reshape(B, S, KV_LORA)], 1)[None]
    kr_p = jnp.concatenate([bc(kr_e[:N_META]), kr_r.reshape(B, S, A_ROPE)], 1)[None]
    return (y_real.reshape(B, S, D_MODEL), y_dec.reshape(Bd, Td, D_MODEL), kv_p, kr_p,
            mlstm_c_p, mlstm_n_p, mlstm_m_p,
            ckv_e[N_META:].reshape(1, Bd, Td, KV_LORA), kr_e[N_META:].reshape(1, Bd, Td, A_ROPE),
            c_d.reshape(1, Bd, M_HEADS, M_DQK, M_DV), n_d.reshape(1, Bd, M_HEADS, M_DQK),
            m_d.reshape(1, Bd, M_HEADS))
```

```python
import functools
import math

import numpy as np
import jax
import jax.numpy as jnp
from jax import lax
from jax.experimental import pallas as pl
from jax.experimental.pallas import tpu as pltpu

F32 = jnp.float32
BF16 = jnp.bfloat16

D_MODEL = 1024
N_META = 16
M_HEADS = 4
M_DQK = 64
M_DV = 128
A_HEADS = 8
A_NOPE = 64
A_ROPE = 32
A_DV = 64
Q_LORA = 256
KV_LORA = 256
ROPE_BASE = 10000.0
PAGE_SIZE = 128
D_FF = 2816
DEPTH = 1
ALPHA = (2.0 * DEPTH) ** 0.25
LN_EPS = 1e-5
RMS_EPS = 1e-6
ATTN_SCALE = (A_NOPE + A_ROPE) ** -0.5
LOG2E = math.log2(math.e)

LANES = 128
HEAD_TILE = LANES
FF_CHUNK = 256
MLSTM_CHUNK = 256
ATTN_BLOCK = 1024
V_ROWS = 80
DEC_SUB = 8
DEC_CHUNK = 32
DEC_SLOTS = 4
VMEM_LIMIT = 56 * 1024 * 1024

_C_Q = 0
_C_V = _C_Q + M_HEADS * HEAD_TILE
_C_O = _C_V + M_HEADS * M_DV
_C_CQ = _C_O + M_HEADS * M_DV
_C_CKV = _C_CQ + Q_LORA
_C_G = _C_CKV + KV_LORA
_C_KRA = _C_G + LANES
_C_KRB = _C_KRA + LANES
_C_END = _C_KRB + LANES


def _cparams(sem):
    return pltpu.CompilerParams(dimension_semantics=sem, vmem_limit_bytes=VMEM_LIMIT)


def _const_spec(shape):
    nd = len(shape)
    return pl.BlockSpec(shape, lambda *_: (0,) * nd, pipeline_mode=pl.Buffered(1))


def _layer_norm(r, g, b):
    mu = jnp.mean(r, axis=-1, keepdims=True)
    d = r - mu
    var = jnp.mean(d * d, axis=-1, keepdims=True)
    return d * lax.rsqrt(var + LN_EPS) * g + b


def _rms_norm(x, g):
    return x * lax.rsqrt(jnp.mean(x * x, axis=-1, keepdims=True) + RMS_EPS) * g


def _dot(a, b):
    return jnp.dot(a, b, preferred_element_type=F32)


def _dot_nt(a, b):
    return lax.dot_general(a, b, (((1,), (1,)), ((), ())), preferred_element_type=F32)


def _swiglu_ln(x, w1_ref, w3_ref, w2_ref, g, b):
    xb = x.astype(BF16)
    y = None
    for j in range(D_FF // FF_CHUNK):
        sl = slice(j * FF_CHUNK, (j + 1) * FF_CHUNK)
        a = _dot(xb, w1_ref[:, sl])
        c = _dot(xb, w3_ref[:, sl])
        hj = (a * jax.nn.sigmoid(a) * c).astype(BF16)
        yj = _dot(hj, w2_ref[sl, :])
        y = yj if y is None else y + yj
    return _layer_norm(ALPHA * x + 0.5 * y, g, b)


def _cast_kernel(*refs):
    n = len(refs) // 2
    for src, dst in zip(refs[:n], refs[n:]):
        dst[...] = src[...].astype(dst.dtype)


def _to_bf16(arrays, steps=8):
    specs = [pl.BlockSpec((a.shape[0] // steps, a.shape[1]), lambda i: (i, 0)) for a in arrays]
    return pl.pallas_call(
        _cast_kernel,
        grid=(steps,),
        in_specs=specs,
        out_specs=specs,
        out_shape=[jax.ShapeDtypeStruct(a.shape, BF16) for a in arrays],
        compiler_params=_cparams(("parallel",)),
        name="cast_bf16",
    )(*arrays)


def _ffn_kernel(x_ref, w1_ref, w3_ref, w2_ref, g_ref, b_ref, o_ref):
    o_ref[...] = _swiglu_ln(x_ref[...], w1_ref, w3_ref, w2_ref, g_ref[...], b_ref[...])


def _ffn_ln(x, w1, w3, w2, g, b, tm):
    n = x.shape[0]
    row = pl.BlockSpec((tm, D_MODEL), lambda i: (i, 0))
    return pl.pallas_call(
        _ffn_kernel,
        grid=(n // tm,),
        in_specs=[row, _const_spec(w1.shape), _const_spec(w3.shape), _const_spec(w2.shape),
                  _const_spec(g.shape), _const_spec(b.shape)],
        out_specs=row,
        out_shape=jax.ShapeDtypeStruct((n, D_MODEL), F32),
        compiler_params=_cparams(("parallel",)),
        name="ffn_ln",
    )(x, w1, w3, w2, g, b)


def _mixin_kernel(x_ref, w_ref, wt_ref, wgt_ref, bg_ref, bgt_ref, qg_ref, wqt_ref, kvg_ref, wuk_ref, wuvt_ref,
                  cos_ref, sin_ref, cost_ref, sint_ref,
                  q_ref, kt_ref, v_ref, o_ref, g_ref, gt_ref, aqt_ref, ak_ref, avt_ref, ckv_ref, kr_ref):
    xb = x_ref[...].astype(BF16)
    z = _dot(xb, w_ref[...])
    q_ref[...] = z[:, _C_Q:_C_V].astype(BF16)
    v_ref[...] = z[:, _C_V:_C_O].astype(BF16)
    o_ref[...] = z[:, _C_O:_C_CQ]
    def gates(pre, is_forget):
        logsig = jnp.minimum(pre, 0.0) - jnp.log1p(jnp.exp(-jnp.abs(pre)))
        return jnp.where(is_forget, logsig, pre)
    gc = z[:, _C_G:_C_G + 2 * M_HEADS] + bg_ref[...]
    g_ref[...] = gates(gc, lax.broadcasted_iota(jnp.int32, gc.shape, 1) >= M_HEADS)
    gr = _dot_nt(wgt_ref[...], xb)[:2 * M_HEADS] + bgt_ref[...]
    gt_ref[...] = gates(gr, lax.broadcasted_iota(jnp.int32, gr.shape, 0) >= M_HEADS)
    kt_ref[...] = _dot_nt(wt_ref[...], xb).astype(BF16)
    cq = _rms_norm(z[:, _C_CQ:_C_CKV], qg_ref[...]).astype(BF16)
    qat = _dot_nt(wqt_ref[...], cq)
    ckv = _rms_norm(z[:, _C_CKV:_C_G], kvg_ref[...])
    ckv_ref[...] = ckv
    ckv_b = ckv.astype(BF16)
    kn = _dot(ckv_b, wuk_ref[...])
    vt = _dot_nt(wuvt_ref[...], ckv_b)
    kr = z[:, _C_KRA:_C_KRB] * cos_ref[...] + z[:, _C_KRB:_C_END] * sin_ref[...]
    kr_ref[...] = kr[:, :A_ROPE]
    cost = cost_ref[...]
    sint = sint_ref[...]
    ones_row = (lax.broadcasted_iota(jnp.int32, (V_ROWS, 1), 0) == A_DV).astype(F32)
    nh = A_HEADS * HEAD_TILE
    for h in range(A_HEADS):
        sl = slice(h * HEAD_TILE, (h + 1) * HEAD_TILE)
        sw = slice(nh + h * HEAD_TILE, nh + (h + 1) * HEAD_TILE)
        aqt_ref[h] = ((qat[sl] * cost + qat[sw] * sint) * (ATTN_SCALE * LOG2E)).astype(BF16)
        ak_ref[h] = (kn[:, sl] + kr).astype(BF16)
        avt_ref[h] = (vt[h * V_ROWS:(h + 1) * V_ROWS] + ones_row).astype(BF16)


def _mixer_inputs(x, wts, tables, tm, n_tab_blocks):
    n = x.shape[0]
    cos, sin = tables
    row = lambda w: pl.BlockSpec((tm, w), lambda i: (i, 0))
    colb = lambda r: pl.BlockSpec((r, tm), lambda i: (0, i))
    head = pl.BlockSpec((A_HEADS, tm, HEAD_TILE), lambda i: (0, i, 0))
    head_t = lambda r: pl.BlockSpec((A_HEADS, r, tm), lambda i: (0, 0, i))
    tab = pl.BlockSpec((tm, HEAD_TILE), lambda i: (i % n_tab_blocks, 0))
    tab_t = pl.BlockSpec((HEAD_TILE, tm), lambda i: (0, i % n_tab_blocks))
    consts = [wts["w_main"], wts["w_kt"], wts["w_gt"], wts["bg_row"], wts["bg_col"], wts["q_norm_g"],
              wts["w_q2t"], wts["kv_norm_g"], wts["w_uk_tiles"], wts["w_uvt"]]
    hq = M_HEADS * HEAD_TILE
    hv = M_HEADS * M_DV
    out_shape = (
        jax.ShapeDtypeStruct((n, hq), BF16),
        jax.ShapeDtypeStruct((hq, n), BF16),
        jax.ShapeDtypeStruct((n, hv), BF16),
        jax.ShapeDtypeStruct((n, hv), F32),
        jax.ShapeDtypeStruct((n, 2 * M_HEADS), F32),
        jax.ShapeDtypeStruct((2 * M_HEADS, n), F32),
        jax.ShapeDtypeStruct((A_HEADS, HEAD_TILE, n), BF16),
        jax.ShapeDtypeStruct((A_HEADS, n, HEAD_TILE), BF16),
        jax.ShapeDtypeStruct((A_HEADS, V_ROWS, n), BF16),
        jax.ShapeDtypeStruct((n, KV_LORA), F32),
        jax.ShapeDtypeStruct((n, A_ROPE), F32),
    )
    out_specs = (row(hq), colb(hq), row(hv), row(hv), row(2 * M_HEADS), colb(2 * M_HEADS),
                 head_t(HEAD_TILE), head, head_t(V_ROWS), row(KV_LORA), row(A_ROPE))
    return pl.pallas_call(
        _mixin_kernel,
        grid=(n // tm,),
        in_specs=[row(D_MODEL)] + [_const_spec(c.shape) for c in consts] + [tab, tab, tab_t, tab_t],
        out_specs=out_specs,
        out_shape=out_shape,
        compiler_params=_cparams(("parallel",)),
        name="mixer_inputs",
    )(x, *consts, cos, sin, cos.T, sin.T)


def _split3(x):
    hi = x.astype(BF16)
    r1 = x - hi.astype(F32)
    mid = r1.astype(BF16)
    lo = (r1 - mid.astype(F32)).astype(BF16)
    return hi, mid, lo


def _mlstm_kernel(q_ref, kt_ref, v_ref, g_ref, gt_ref, c0_ref, m0_ref, h_ref, c_ref, m_ref, c_sc, m_sc,
                  *, chunk):
    ci = pl.program_id(1)

    @pl.when(ci == 0)
    def _():
        c_sc[...] = c0_ref[0]
        m_sc[...] = m0_ref[0]

    L = chunk
    row = lax.broadcasted_iota(jnp.int32, (L, L), 0)
    col = lax.broadcasted_iota(jnp.int32, (L, L), 1)
    causal = row >= col
    tril = causal.astype(BF16)
    triu = (row <= col).astype(BF16)
    g = g_ref[...]
    gt = gt_ref[...]
    cum_c = sum(_dot(tril, p) for p in _split3(g))
    cum_r = sum(_dot(p, triu) for p in _split3(gt))
    ones_col = (lax.broadcasted_iota(jnp.int32, (L, LANES), 1) == 0).astype(BF16)
    for h in range(M_HEADS):
        sl = slice(h * HEAD_TILE, (h + 1) * HEAD_TILE)
        qh = q_ref[:, sl]
        kth = kt_ref[sl, :]
        v_aug = jnp.concatenate([v_ref[:, h * M_DV:(h + 1) * M_DV], ones_col], axis=1)
        ig_r = gt[h:h + 1, :]
        lf_r = gt[M_HEADS + h:M_HEADS + h + 1, :]
        cr = cum_r[M_HEADS + h:M_HEADS + h + 1, :]
        cc = cum_c[:, M_HEADS + h:M_HEADS + h + 1]
        m_run = m_sc[h, 0:1, 0:1]
        d = jnp.where(causal, cc - cr + ig_r, -jnp.inf)
        inter = cc + m_run
        m_t = jnp.maximum(inter, jnp.max(d, axis=1, keepdims=True))
        s = (_dot(qh, kth) * jnp.exp(d - m_t)).astype(BF16)
        w_inter = jnp.exp(inter - m_t)
        c_state = c_sc[h]
        r = _dot(s, v_aug) + w_inter * _dot(qh, c_state.astype(BF16))
        qn = r[:, M_DV:M_DV + 1]
        den = jnp.maximum(jnp.abs(qn), jnp.exp(-m_t))
        h_ref[:, h * M_DV:(h + 1) * M_DV] = r[:, :M_DV] / den
        last = jnp.sum(lf_r, axis=1, keepdims=True)
        g_r = last - cr + ig_r
        m_new = jnp.maximum(last + m_run, jnp.max(g_r, axis=1, keepdims=True))
        decay = jnp.exp(last + m_run - m_new)
        wkt = (kth.astype(F32) * jnp.exp(g_r - m_new)).astype(BF16)
        c_sc[h] = decay * c_state + _dot(wkt, v_aug)
        m_sc[h] = jnp.broadcast_to(m_new, (8, LANES))

    @pl.when(ci == pl.num_programs(1) - 1)
    def _():
        c_ref[0] = c_sc[...]
        m_ref[0] = m_sc[...]


def _mlstm_scan(q, kt, v, g, gt, c0, m0, batch, seq, chunk):
    nc = seq // chunk
    hq = M_HEADS * HEAD_TILE
    hv = M_HEADS * M_DV
    row = lambda w: pl.BlockSpec((chunk, w), lambda b, c: (b * nc + c, 0))
    colb = lambda r: pl.BlockSpec((r, chunk), lambda b, c: (0, b * nc + c))
    st_in = lambda s: pl.BlockSpec((1,) + s, lambda b, c: (0,) * (len(s) + 1))
    st_out = lambda s: pl.BlockSpec((1,) + s, lambda b, c: (b,) + (0,) * len(s))
    cs, ms = (M_HEADS, HEAD_TILE, 2 * LANES), (M_HEADS, 8, LANES)
    return pl.pallas_call(
        functools.partial(_mlstm_kernel, chunk=chunk),
        grid=(batch, nc),
        in_specs=[row(hq), colb(hq), row(hv), row(2 * M_HEADS), colb(2 * M_HEADS), st_in(cs), st_in(ms)],
        out_specs=(row(hv), st_out(cs), st_out(ms)),
        out_shape=(jax.ShapeDtypeStruct((batch * seq, hv), F32),
                   jax.ShapeDtypeStruct((batch,) + cs, F32),
                   jax.ShapeDtypeStruct((batch,) + ms, F32)),
        scratch_shapes=[pltpu.VMEM(cs, F32), pltpu.VMEM(ms, F32)],
        compiler_params=_cparams(("parallel", "arbitrary")),
        name="mlstm_scan",
    )(q, kt, v, g, gt, c0, m0)


def _mlstm_step_kernel(q_ref, k_ref, kt_ref, v_ref, gc_ref, gr_ref, c0_ref, n0_ref, m0_ref,
                       h_ref, c_ref, n_ref, m_ref):
    T = q_ref.shape[1]
    row = lax.broadcasted_iota(jnp.int32, (T, T), 0)
    col = lax.broadcasted_iota(jnp.int32, (T, T), 1)
    causal = (row >= col)[None]
    q = q_ref[...]
    k = k_ref[...]
    v = v_ref[...]
    ig_c, lf_c = gc_ref[:, :, 0:1], gc_ref[:, :, 1:2]
    ig_r, lf_r = gr_ref[:, 0:1, :], gr_ref[:, 1:2, :]
    m0 = m0_ref[...]
    c0 = c0_ref[...]
    n0 = n0_ref[...]
    cum_c = jnp.sum(jnp.where(causal, lf_r, 0.0), axis=2, keepdims=True)
    cum_r = jnp.sum(jnp.where((row <= col)[None], lf_c, 0.0), axis=1, keepdims=True)
    d = jnp.where(causal, cum_c - cum_r + ig_r, -jnp.inf)
    inter = cum_c + m0
    m_t = jnp.maximum(inter, jnp.max(d, axis=2, keepdims=True))
    qk = jnp.einsum("gtd,gsd->gts", q, k, preferred_element_type=F32)
    s = qk * jnp.exp(d - m_t)
    w_inter = jnp.exp(inter - m_t)
    num = (jnp.einsum("gts,gsv->gtv", s.astype(BF16), v, preferred_element_type=F32)
           + w_inter * jnp.einsum("gtd,gdv->gtv", q, c0.astype(BF16), preferred_element_type=F32))
    qn = jnp.sum(s, axis=2, keepdims=True) + w_inter * jnp.sum(q.astype(F32) * n0, axis=2, keepdims=True)
    h_ref[...] = num / jnp.maximum(jnp.abs(qn), jnp.exp(-m_t))
    last = cum_c[:, T - 1:T, :]
    g_c = last - cum_c + ig_c
    g_r = last - cum_r + ig_r
    m_new = jnp.maximum(last + m0, jnp.max(g_c, axis=1, keepdims=True))
    decay = jnp.exp(last + m0 - m_new)
    wk = k.astype(F32) * jnp.exp(g_c - m_new)
    wkt = (kt_ref[...].astype(F32) * jnp.exp(g_r - m_new)).astype(BF16)
    c_ref[...] = decay * c0 + jnp.einsum("gds,gsv->gdv", wkt, v, preferred_element_type=F32)
    n_ref[...] = decay * n0 + jnp.sum(wk, axis=1, keepdims=True)
    m_ref[...] = m_new


def _mlstm_step(q, k, kt, v, gc, gr, c0, n0, m0, gb):
    gtot, T = q.shape[0], q.shape[1]
    blk = lambda a, b: pl.BlockSpec((gb, a, b), lambda i: (i, 0, 0))
    return pl.pallas_call(
        _mlstm_step_kernel,
        grid=(gtot // gb,),
        in_specs=[blk(T, M_DQK), blk(T, M_DQK), blk(M_DQK, T), blk(T, M_DV), blk(T, 2), blk(2, T),
                  blk(M_DQK, M_DV), blk(1, M_DQK), blk(1, 1)],
        out_specs=(blk(T, M_DV), blk(M_DQK, M_DV), blk(1, M_DQK), blk(1, 1)),
        out_shape=(jax.ShapeDtypeStruct((gtot, T, M_DV), F32),
                   jax.ShapeDtypeStruct((gtot, M_DQK, M_DV), F32),
                   jax.ShapeDtypeStruct((gtot, 1, M_DQK), F32),
                   jax.ShapeDtypeStruct((gtot, 1, 1), F32)),
        compiler_params=_cparams(("parallel",)),
        name="mlstm_step",
    )(q, k, kt, v, gc, gr, c0, n0, m0)


def _attn_kernel(qi_tab, ki_tab, qt_ref, k_ref, vt_ref, km_ref, vmt_ref, o_ref, acc_sc, m_sc):
    step = pl.program_id(1)
    qi = qi_tab[step]
    ki = ki_tab[step]
    tq = qt_ref.shape[2]

    @pl.when(ki == 0)
    def _():
        ss = [_dot(km_ref[h], qt_ref[h]) for h in range(A_HEADS)]
        ms = [jnp.max(s, axis=0, keepdims=True) for s in ss]
        for h in range(A_HEADS):
            acc_sc[h] = _dot(vmt_ref[h], jnp.exp2(ss[h] - ms[h]).astype(BF16))
            m_sc[h] = ms[h]

    @pl.when(ki < qi)
    def _():
        s_next = _dot(k_ref[0], qt_ref[0])
        for h in range(A_HEADS):
            s = s_next
            if h + 1 < A_HEADS:
                s_next = _dot(k_ref[h + 1], qt_ref[h + 1])
            m_prev = m_sc[h]
            m_new = jnp.maximum(m_prev, jnp.max(s, axis=0, keepdims=True))
            p = jnp.exp2(s - m_new).astype(BF16)
            acc_sc[h] = jnp.exp2(m_prev - m_new) * acc_sc[h] + _dot(vt_ref[h], p)
            m_sc[h] = m_new

    def update_diagonal():
        hk = tq // 2
        row = lax.broadcasted_iota(jnp.int32, (hk, tq), 0)
        col = lax.broadcasted_iota(jnp.int32, (hk, tq), 1)
        keep_a = row <= col
        keep_b = keep_a[:, :hk]
        scores_a = lambda h: _dot(k_ref[h, :hk], qt_ref[h])
        s_next = scores_a(0)
        for h in range(A_HEADS):
            s1 = jnp.where(keep_a, s_next, -jnp.inf)
            s2 = jnp.where(keep_b, _dot(k_ref[h, hk:], qt_ref[h, :, hk:]), -jnp.inf)
            if h + 1 < A_HEADS:
                s_next = scores_a(h + 1)
            m_prev = m_sc[h]
            m1 = jnp.maximum(m_prev, jnp.max(s1, axis=0, keepdims=True))
            m2 = jnp.maximum(m1[:, hk:], jnp.max(s2, axis=0, keepdims=True))
            m_new = jnp.concatenate([m1[:, :hk], m2], axis=1)
            p1 = jnp.exp2(s1 - m_new).astype(BF16)
            p2 = jnp.exp2(s2 - m2).astype(BF16)
            pv = _dot(vt_ref[h, :, :hk], p1)
            pv = pv + jnp.concatenate([jnp.zeros((V_ROWS, hk), F32), _dot(vt_ref[h, :, hk:], p2)], axis=1)
            acc_sc[h] = jnp.exp2(m_prev - m_new) * acc_sc[h] + pv
            m_sc[h] = m_new

    @pl.when(ki == qi)
    def _():
        update_diagonal()
        for j in range(A_HEADS // 2):
            ae = acc_sc[2 * j]
            ao = acc_sc[2 * j + 1]
            pair = jnp.concatenate([ae[:A_DV] / ae[A_DV:A_DV + 1], ao[:A_DV] / ao[A_DV:A_DV + 1]], axis=0)
            o_ref[:, j * HEAD_TILE:(j + 1) * HEAD_TILE] = pair.T.astype(o_ref.dtype)


def _attention_prompt(aqt, ak, avt, k_meta, vt_meta, batch, seq):
    blk = ATTN_BLOCK
    nq = seq // blk
    qi_np, ki_np = zip(*[(i, j) for i in range(nq) for j in range(i + 1)])
    qi_tab = jnp.asarray(np.array(qi_np, np.int32))
    ki_tab = jnp.asarray(np.array(ki_np, np.int32))
    rows = lambda tab: pl.BlockSpec((A_HEADS, blk, HEAD_TILE),
                                    lambda b, s, qt, kt: (0, b * nq + (qt, kt)[tab][s], 0))
    cols = lambda tab, r: pl.BlockSpec((A_HEADS, r, blk),
                                       lambda b, s, qt, kt: (0, 0, b * nq + (qt, kt)[tab][s]))
    grid_spec = pltpu.PrefetchScalarGridSpec(
        num_scalar_prefetch=2,
        grid=(batch, len(qi_np)),
        in_specs=[cols(0, HEAD_TILE), rows(1), cols(1, V_ROWS),
                  pl.BlockSpec((A_HEADS, N_META, HEAD_TILE), lambda b, s, qt, kt: (0, 0, 0)),
                  pl.BlockSpec((A_HEADS, V_ROWS, N_META), lambda b, s, qt, kt: (0, 0, 0))],
        out_specs=pl.BlockSpec((blk, A_HEADS * A_DV), lambda b, s, qt, kt: (b * nq + qt[s], 0)),
        scratch_shapes=[pltpu.VMEM((A_HEADS, V_ROWS, blk), F32), pltpu.VMEM((A_HEADS, 1, blk), F32)],
    )
    return pl.pallas_call(
        _attn_kernel,
        grid_spec=grid_spec,
        out_shape=jax.ShapeDtypeStruct((batch * seq, A_HEADS * A_DV), BF16),
        compiler_params=_cparams(("parallel", "arbitrary")),
        name="attn_prompt",
    )(qi_tab, ki_tab, aqt, ak, avt, k_meta, vt_meta)


def _absorb_kernel(qt_ref, wuk_ref, o_ref):
    for h in range(A_HEADS):
        o_ref[h] = _dot(wuk_ref[h], qt_ref[h]).astype(BF16)


def _absorb_q(aqt, wuk_tiles):
    n = aqt.shape[2]
    return pl.pallas_call(
        _absorb_kernel,
        out_shape=jax.ShapeDtypeStruct((A_HEADS, KV_LORA, n), BF16),
        compiler_params=pltpu.CompilerParams(vmem_limit_bytes=VMEM_LIMIT),
        name="absorb_q",
    )(aqt, wuk_tiles)


def _decode_attn_kernel(pt_ref, ql_ref, qr_ref, cn_ref, kn_ref, lat_hbm, krt_hbm, o_ref,
                        lat_buf, krt_buf, sem_lat, sem_krt):
    b = pl.program_id(0)
    nb = pl.num_programs(0)
    sub, chunk, n_slots = DEC_SUB, DEC_CHUNK, DEC_SLOTS
    n_sub = pt_ref.shape[1] // sub
    spc = chunk // sub
    n_chunks = n_sub // spc
    ahead = 2 * spc
    assert n_chunks % n_slots == 0 and n_slots >= 4

    def copies(bb, j):
        slot = (j // spc) % n_slots
        for k in range((j % spc) * sub, (j % spc + 1) * sub):
            page = pt_ref[bb, (j // spc) * chunk + k]
            yield pltpu.make_async_copy(lat_hbm.at[page], lat_buf.at[slot, k], sem_lat.at[slot])
            yield pltpu.make_async_copy(krt_hbm.at[page], krt_buf.at[slot, k], sem_krt.at[slot])

    def start(bb, j):
        for cp in copies(bb, j):
            cp.start()

    def wait_chunk(c):
        for j in range(c * spc, (c + 1) * spc):
            for cp in copies(b, j):
                cp.wait()

    @pl.when(b == 0)
    def _():
        for j in range(ahead):
            start(0, j)

    ql = ql_ref[0]
    qr = qr_ref[0]
    rows = ql.shape[0]
    m_run = jnp.full((rows, 1), -jnp.inf, F32)
    l_run = jnp.zeros((rows, 1), F32)
    acc = jnp.zeros((rows, KV_LORA), F32)

    def scores(j):
        slot = (j // spc) % n_slots
        ks = range((j % spc) * sub, (j % spc + 1) * sub)
        return jnp.concatenate(
            [_dot_nt(ql, lat_buf[slot, k].astype(BF16)) + _dot(qr, krt_buf[slot, k].astype(BF16)) for k in ks],
            axis=1)

    def weighted_values(j, pb):
        slot = (j // spc) % n_slots
        ks = range((j % spc) * sub, (j % spc + 1) * sub)
        return sum(_dot(pb[:, n * PAGE_SIZE:(n + 1) * PAGE_SIZE], lat_buf[slot, k].astype(BF16))
                   for n, k in enumerate(ks))

    wait_chunk(0)
    s_next = scores(0)
    pending = None
    for i in range(n_sub):
        s = s_next
        if i + 1 < n_sub:
            if (i + 1) % spc == 0:
                wait_chunk((i + 1) // spc)
            s_next = scores(i + 1)
        j = i + ahead
        if j < n_sub:
            start(b, j)
        else:
            @pl.when(b + 1 < nb)
            def _():
                start(b + 1, j - n_sub)
        if pending is not None:
            jp, pb, alpha = pending
            acc = alpha * acc + weighted_values(jp, pb)
        m_new = jnp.maximum(m_run, jnp.max(s, axis=1, keepdims=True))
        p = jnp.exp2(s - m_new)
        alpha = jnp.exp2(m_run - m_new)
        l_run = alpha * l_run + jnp.sum(p, axis=1, keepdims=True)
        pending = (i, p.astype(BF16), alpha)
        m_run = m_new
    jp, pb, alpha = pending
    acc = alpha * acc + weighted_values(jp, pb)

    T = cn_ref.shape[1]
    cn = cn_ref[0].astype(BF16).astype(F32)
    kn = kn_ref[0].astype(BF16).astype(F32)
    qlf = ql.astype(F32)
    qrf = qr.astype(F32)
    t_of_row = lax.broadcasted_iota(jnp.int32, (rows, 1), 0) % T
    s_new = []
    for t in range(T):
        st = (jnp.sum(qlf * cn[t:t + 1], axis=1, keepdims=True)
              + jnp.sum(qrf * kn[t:t + 1], axis=1, keepdims=True))
        s_new.append(jnp.where(t_of_row >= t, st, -jnp.inf))
    m_fin = m_run
    for st in s_new:
        m_fin = jnp.maximum(m_fin, st)
    alpha = jnp.exp2(m_run - m_fin)
    acc = alpha * acc
    l_run = alpha * l_run
    for t, st in enumerate(s_new):
        pt = jnp.exp2(st - m_fin)
        l_run = l_run + pt
        acc = acc + pt * cn[t:t + 1]
    o_ref[0] = acc / l_run


def _decode_attention(page_table, q_lat, q_rope, c_new, kr_new, cache_lat, cache_krt):
    nb = page_table.shape[0]
    rows = q_lat.shape[1]
    per_b = lambda a, b: pl.BlockSpec((1, a, b), lambda i, pt: (i, 0, 0))
    hbm = pl.BlockSpec(memory_space=pl.ANY)
    grid_spec = pltpu.PrefetchScalarGridSpec(
        num_scalar_prefetch=1,
        grid=(nb,),
        in_specs=[per_b(rows, KV_LORA), per_b(rows, A_ROPE), per_b(c_new.shape[1], KV_LORA),
                  per_b(kr_new.shape[1], A_ROPE), hbm, hbm],
        out_specs=per_b(rows, KV_LORA),
        scratch_shapes=[pltpu.VMEM((DEC_SLOTS, DEC_CHUNK, PAGE_SIZE, KV_LORA), F32),
                        pltpu.VMEM((DEC_SLOTS, DEC_CHUNK, A_ROPE, PAGE_SIZE), F32),
                        pltpu.SemaphoreType.DMA((DEC_SLOTS,)), pltpu.SemaphoreType.DMA((DEC_SLOTS,))],
    )
    return pl.pallas_call(
        _decode_attn_kernel,
        grid_spec=grid_spec,
        out_shape=jax.ShapeDtypeStruct((nb, rows, KV_LORA), F32),
        compiler_params=_cparams(("arbitrary",)),
        name="attn_decode",
    )(page_table, q_lat, q_rope, c_new, kr_new, cache_lat, cache_krt)


def _lat2attn_kernel(o_ref, wuv_ref, a_ref):
    for j in range(A_HEADS // 2):
        pair = (_dot(o_ref[2 * j].astype(BF16), wuv_ref[2 * j])
                + _dot(o_ref[2 * j + 1].astype(BF16), wuv_ref[2 * j + 1]))
        a_ref[:, j * HEAD_TILE:(j + 1) * HEAD_TILE] = pair.astype(a_ref.dtype)


def _lat2attn(o_lat, wuv_tiles):
    n = o_lat.shape[1]
    return pl.pallas_call(
        _lat2attn_kernel,
        out_shape=jax.ShapeDtypeStruct((n, A_HEADS * A_DV), BF16),
        compiler_params=pltpu.CompilerParams(vmem_limit_bytes=VMEM_LIMIT),
        name="lat2attn",
    )(o_lat, wuv_tiles)


def _mixout_kernel(x_ref, hm_ref, om_ref, at_ref, ng_ref, wo_ref, g2_ref, b2_ref,
                   w1_ref, w3_ref, w2_ref, g3_ref, b3_ref, o_ref):
    x = x_ref[...]
    parts = []
    for h in range(M_HEADS):
        sl = slice(h * M_DV, (h + 1) * M_DV)
        hn = _rms_norm(hm_ref[:, sl], ng_ref[:, sl])
        parts.append((hn * jax.nn.sigmoid(om_ref[:, sl])).astype(BF16))
    cat = jnp.concatenate(parts + [at_ref[...]], axis=1)
    x2 = _layer_norm(ALPHA * x + _dot(cat, wo_ref[...]), g2_ref[...], b2_ref[...])
    o_ref[...] = _swiglu_ln(x2, w1_ref, w3_ref, w2_ref, g3_ref[...], b3_ref[...])


def _mixer_output_ffn(x, hm, om, attn, wts, tm):
    n = x.shape[0]
    row = lambda w: pl.BlockSpec((tm, w), lambda i: (i, 0))
    consts = [wts["mlstm_norm_g"], wts["w_out"], wts["ln2_g"], wts["ln2_b"],
              wts["ffn2_w1"], wts["ffn2_w3"], wts["ffn2_w2"], wts["ln3_g"], wts["ln3_b"]]
    hv = M_HEADS * M_DV
    return pl.pallas_call(
        _mixout_kernel,
        grid=(n // tm,),
        in_specs=[row(D_MODEL), row(hv), row(hv), row(A_HEADS * A_DV)] + [_const_spec(c.shape) for c in consts],
        out_specs=row(D_MODEL),
        out_shape=jax.ShapeDtypeStruct((n, D_MODEL), F32),
        compiler_params=_cparams(("parallel",)),
        name="mixer_output_ffn",
    )(x, hm, om, attn, *consts)


def _pack_weights(w_in, b_gates, mlstm_norm_g, q_norm_g, w_uq, kv_norm_g, w_uk, w_uv, w_out):
    o_q, o_k, o_v, o_o = 0, M_HEADS * M_DQK, 2 * M_HEADS * M_DQK, 2 * M_HEADS * M_DQK + M_HEADS * M_DV
    o_g = o_o + M_HEADS * M_DV
    o_cq = o_g + 2 * M_HEADS
    o_ckv = o_cq + Q_LORA
    o_kr = o_ckv + KV_LORA
    half = A_ROPE // 2
    dq = A_NOPE + A_ROPE

    def pad_axis(a, axis, lo, size):
        pads = [(0, 0)] * a.ndim
        pads[axis] = (lo, size - lo - a.shape[axis])
        return jnp.pad(a, pads)

    q_cols = pad_axis(w_in[:, o_q:o_k].reshape(D_MODEL, M_HEADS, M_DQK), 2, 0, HEAD_TILE).reshape(D_MODEL, -1)
    kr = w_in[:, o_kr:o_kr + A_ROPE]
    kr_sw = jnp.concatenate([kr[:, half:], kr[:, :half]], 1)
    w_main = jnp.concatenate(
        [q_cols, w_in[:, o_v:o_g], w_in[:, o_cq:o_kr], pad_axis(w_in[:, o_g:o_cq], 1, 0, LANES),
         pad_axis(kr, 1, 0, LANES), pad_axis(kr_sw, 1, 0, LANES)], 1)
    assert w_main.shape[1] == _C_END
    w_kt = pad_axis((w_in[:, o_k:o_v].T * (M_DQK ** -0.5)).reshape(M_HEADS, M_DQK, D_MODEL), 1, 0, HEAD_TILE)
    w_kt = w_kt.reshape(M_HEADS * HEAD_TILE, D_MODEL)
    w_gt = pad_axis(w_in[:, o_g:o_cq].T, 0, 0, 16)
    uq = w_uq.reshape(Q_LORA, A_HEADS, dq)
    q_tiles = pad_axis(jnp.concatenate([uq[..., A_NOPE:], uq[..., :A_NOPE]], -1), 2, 0, HEAD_TILE)
    q_sw_tiles = pad_axis(jnp.concatenate([uq[..., A_NOPE + half:], uq[..., A_NOPE:A_NOPE + half]], -1),
                          2, 0, HEAD_TILE)
    w_q2 = jnp.concatenate([q_tiles.reshape(Q_LORA, -1), q_sw_tiles.reshape(Q_LORA, -1)], 1)
    uk_tiles = pad_axis(w_uk, 2, A_ROPE, HEAD_TILE)
    w_uvt = pad_axis(w_uv.transpose(1, 2, 0), 1, 0, V_ROWS).reshape(A_HEADS * V_ROWS, KV_LORA)
    uv_pairs = w_uv.reshape(KV_LORA, A_HEADS // 2, 2, A_DV)
    uv_heads = jnp.stack([pad_axis(uv_pairs[:, :, 0], 2, 0, HEAD_TILE),
                          pad_axis(uv_pairs[:, :, 1], 2, A_DV, HEAD_TILE)], 2)
    bf = lambda a: a.astype(BF16)
    return {
        "w_main": bf(w_main), "w_kt": bf(w_kt), "w_gt": bf(w_gt),
        "bg_row": b_gates.reshape(1, -1), "bg_col": b_gates.reshape(-1, 1),
        "q_norm_g": q_norm_g.reshape(1, -1), "kv_norm_g": kv_norm_g.reshape(1, -1),
        "w_q2t": bf(w_q2.T),
        "w_uk_tiles": bf(uk_tiles.reshape(KV_LORA, -1)),
        "w_uvt": bf(w_uvt),
        "w_uk_heads": bf(uk_tiles.transpose(1, 0, 2)),
        "w_uv_heads": bf(uv_heads.reshape(KV_LORA, A_HEADS, HEAD_TILE).transpose(1, 0, 2)),
        "mlstm_norm_g": mlstm_norm_g.reshape(1, -1), "w_out": bf(w_out),
    }


def _rope_tables(pos):
    half = A_ROPE // 2
    inv = ROPE_BASE ** (-np.arange(half, dtype=np.float64) / half)
    ang = np.asarray(pos, np.float64)[:, None] * inv[None, :]
    cos, sin = np.cos(ang), np.sin(ang)
    n = ang.shape[0]
    cos_t = np.concatenate([cos, cos, np.ones((n, A_NOPE)), np.zeros((n, HEAD_TILE - A_NOPE - A_ROPE))], 1)
    sin_t = np.concatenate([-sin, sin, np.zeros((n, HEAD_TILE - A_ROPE))], 1)
    return cos_t.astype(np.float32), sin_t.astype(np.float32)


def kernel(x_prompt, x_sample, cache_kv_latent, cache_k_rope, state_mlstm_C, state_mlstm_n, state_mlstm_m, page_table, meta_tokens, ffn1_w1, ffn1_w3, ffn1_w2, ln1_g, ln1_b, w_in, b_gates, mlstm_norm_g, q_norm_g, w_uq, kv_norm_g, w_uk, w_uv, w_out, ln2_g, ln2_b, ffn2_w1, ffn2_w3, ffn2_w2, ln3_g, ln3_b):
    assert w_in.shape[0] == DEPTH == 1
    B, S, _ = x_prompt.shape
    Bd, Td, _ = x_sample.shape
    n_real, n_dec = B * S, Bd * Td
    l = 0
    vec = lambda a: a[l].reshape(1, -1)
    wts = _pack_weights(w_in[l], b_gates[l], mlstm_norm_g[l], q_norm_g[l], w_uq[l], kv_norm_g[l],
                        w_uk[l], w_uv[l], w_out[l])
    ffn_bf = _to_bf16([ffn1_w1[l], ffn1_w3[l], ffn1_w2[l], ffn2_w1[l], ffn2_w3[l], ffn2_w2[l]])
    wts.update({"ln2_g": vec(ln2_g), "ln2_b": vec(ln2_b), "ln3_g": vec(ln3_g), "ln3_b": vec(ln3_b),
                "ffn2_w1": ffn_bf[3], "ffn2_w3": ffn_bf[4], "ffn2_w2": ffn_bf[5]})
    f1 = (ffn_bf[0], ffn_bf[1], ffn_bf[2], vec(ln1_g), vec(ln1_b))

    x_real = x_prompt.reshape(n_real, D_MODEL)
    x_extra = jnp.concatenate([meta_tokens.astype(F32), x_sample.reshape(n_dec, D_MODEL)], 0)
    n_extra = x_extra.shape[0]
    tm = 512

    h_real = _ffn_ln(x_real, *f1, tm)
    h_extra = _ffn_ln(x_extra, *f1, n_extra)

    pos_e = np.concatenate([np.arange(N_META), np.tile(page_table.shape[1] * PAGE_SIZE + np.arange(Td), Bd)])
    (q_r, kt_r, v_r, om_r, g_r, gt_r, aqt_r, ak_r, avt_r, ckv_r, kr_r) = _mixer_inputs(
        h_real, wts, _rope_tables(N_META + np.arange(S)), tm, S // tm)
    (q_e, kt_e, v_e, om_e, g_e, gt_e, aqt_e, ak_e, avt_e, ckv_e, kr_e) = _mixer_inputs(
        h_extra, wts, _rope_tables(pos_e), n_extra, 1)

    cs, ms = (1, M_HEADS, HEAD_TILE, 2 * LANES), (1, M_HEADS, 8, LANES)
    _, c_meta, m_meta = _mlstm_scan(q_e[:N_META], kt_e[:, :N_META], v_e[:N_META], g_e[:N_META],
                                    gt_e[:, :N_META], jnp.zeros(cs, F32), jnp.zeros(ms, F32), 1, N_META, N_META)
    hm_r, c_p, m_p = _mlstm_scan(q_r, kt_r, v_r, g_r, gt_r, c_meta, m_meta, B, S, MLSTM_CHUNK)
    mlstm_c_p = c_p[:, :, :M_DQK, :M_DV][None]
    mlstm_n_p = c_p[:, :, :M_DQK, M_DV][None]
    mlstm_m_p = m_p[:, :, 0, 0][None]

    G = Bd * M_HEADS
    def dec_heads(a, w, used):
        return a.reshape(Bd, Td, M_HEADS, w)[..., :used].transpose(0, 2, 1, 3).reshape(G, Td, used)
    q_d = dec_heads(q_e[N_META:], HEAD_TILE, M_DQK)
    v_d = dec_heads(v_e[N_META:], M_DV, M_DV)
    kt_d = (kt_e[:, N_META:].reshape(M_HEADS, HEAD_TILE, Bd, Td)[:, :M_DQK]
            .transpose(2, 0, 1, 3).reshape(G, M_DQK, Td))
    k_d = kt_d.transpose(0, 2, 1)
    gdec = g_e[N_META:].reshape(Bd, Td, 2, M_HEADS).transpose(0, 3, 1, 2).reshape(G, Td, 2)
    hm_d, c_d, n_d, m_d = _mlstm_step(
        q_d, k_d, kt_d, v_d, gdec, gdec.transpose(0, 2, 1),
        state_mlstm_C[l].reshape(G, M_DQK, M_DV), state_mlstm_n[l].reshape(G, 1, M_DQK),
        state_mlstm_m[l].reshape(G, 1, 1), 64)
    hm_d = hm_d.reshape(Bd, M_HEADS, Td, M_DV).transpose(0, 2, 1, 3).reshape(n_dec, M_HEADS * M_DV)

    attn_r = _attention_prompt(aqt_r, ak_r, avt_r, ak_e[:, :N_META], avt_e[:, :, :N_META], B, S)

    aqt_d = aqt_e[:, :, N_META:]
    q_lat_t = _absorb_q(aqt_d, wts["w_uk_heads"])
    rows = A_HEADS * Td
    by_seq = lambda a: (a.reshape(A_HEADS, a.shape[1], Bd, Td).transpose(2, 0, 3, 1)
                        .reshape(Bd, rows, a.shape[1]))
    o_lat = _decode_attention(
        page_table, by_seq(q_lat_t), by_seq(aqt_d[:, :A_ROPE]),
        ckv_e[N_META:].reshape(Bd, Td, KV_LORA), kr_e[N_META:].reshape(Bd, Td, A_ROPE),
        cache_kv_latent[l], jnp.swapaxes(cache_k_rope[l], 1, 2))
    o_lat = o_lat.reshape(Bd, A_HEADS, Td, KV_LORA).transpose(1, 0, 2, 3).reshape(A_HEADS, n_dec, KV_LORA)
    attn_d = _lat2attn(o_lat, wts["w_uv_heads"])

    y_real = _mixer_output_ffn(h_real, hm_r, om_r, attn_r, wts, tm)
    y_dec = _mixer_output_ffn(h_extra[N_META:], hm_d, om_e[N_META:], attn_d, wts, n_dec)

    bc = lambda a: jnp.broadcast_to(a[None], (B,) + a.shape)
    kv_p = jnp.concatenate([bc(ckv_e[:N_META]), ckv_r.reshape(B, S, KV_LORA)], 1)[None]
    kr_p = jnp.concatenate([bc(kr_e[:N_META]), kr_r.reshape(B, S, A_ROPE)], 1)[None]
    return (y_real.reshape(B, S, D_MODEL), y_dec.reshape(Bd, Td, D_MODEL), kv_p, kr_p,
            mlstm_c_p, mlstm_n_p, mlstm_m_p,
            ckv_e[N_META:].reshape(1, Bd, Td, KV_LORA), kr_e[N_META:].reshape(1, Bd, Td, A_ROPE),
            c_d.reshape(1, Bd, M_HEADS, M_DQK, M_DV), n_d.reshape(1, Bd, M_HEADS, M_DQK),
            m_d.reshape(1, Bd, M_HEADS))
```

```python
import functools
import math

import numpy as np
import jax
import jax.numpy as jnp
from jax import lax
from jax.experimental import pallas as pl
from jax.experimental.pallas import tpu as pltpu

F32 = jnp.float32
BF16 = jnp.bfloat16

D_MODEL = 1024
N_META = 16
M_HEADS = 4
M_DQK = 64
M_DV = 128
A_HEADS = 8
A_NOPE = 64
A_ROPE = 32
A_DV = 64
Q_LORA = 256
KV_LORA = 256
ROPE_BASE = 10000.0
PAGE_SIZE = 128
D_FF = 2816
DEPTH = 1
ALPHA = (2.0 * DEPTH) ** 0.25
LN_EPS = 1e-5
RMS_EPS = 1e-6
ATTN_SCALE = (A_NOPE + A_ROPE) ** -0.5
LOG2E = math.log2(math.e)

LANES = 128
HEAD_TILE = LANES
FF_CHUNK = 256
MLSTM_CHUNK = 256
ATTN_BLOCK = 1024
V_ROWS = 80
DEC_SEQS = 2
DEC_SUB = 8
DEC_CHUNK = 64
DEC_SLOTS = 4
VMEM_LIMIT = 56 * 1024 * 1024

_C_Q = 0
_C_V = _C_Q + M_HEADS * HEAD_TILE
_C_O = _C_V + M_HEADS * M_DV
_C_CQ = _C_O + M_HEADS * M_DV
_C_CKV = _C_CQ + Q_LORA
_C_G = _C_CKV + KV_LORA
_C_KRA = _C_G + LANES
_C_KRB = _C_KRA + LANES
_C_END = _C_KRB + LANES


def _cparams(sem):
    return pltpu.CompilerParams(dimension_semantics=sem, vmem_limit_bytes=VMEM_LIMIT)


def _const_spec(shape):
    nd = len(shape)
    return pl.BlockSpec(shape, lambda *_: (0,) * nd, pipeline_mode=pl.Buffered(1))


def _layer_norm(r, g, b):
    mu = jnp.mean(r, axis=-1, keepdims=True)
    d = r - mu
    var = jnp.mean(d * d, axis=-1, keepdims=True)
    return d * lax.rsqrt(var + LN_EPS) * g + b


def _rms_norm(x, g):
    return x * lax.rsqrt(jnp.mean(x * x, axis=-1, keepdims=True) + RMS_EPS) * g


def _dot(a, b):
    return jnp.dot(a, b, preferred_element_type=F32)


def _dot_nt(a, b):
    return lax.dot_general(a, b, (((1,), (1,)), ((), ())), preferred_element_type=F32)


def _swiglu_ln(x, w1_ref, w3_ref, w2_ref, g, b):
    xb = x.astype(BF16)
    y = None
    for j in range(D_FF // FF_CHUNK):
        sl = slice(j * FF_CHUNK, (j + 1) * FF_CHUNK)
        a = _dot(xb, w1_ref[:, sl])
        c = _dot(xb, w3_ref[:, sl])
        hj = (a * jax.nn.sigmoid(a) * c).astype(BF16)
        yj = _dot(hj, w2_ref[sl, :])
        y = yj if y is None else y + yj
    return _layer_norm(ALPHA * x + 0.5 * y, g, b)


def _cast_kernel(*refs):
    n = len(refs) // 2
    for src, dst in zip(refs[:n], refs[n:]):
        dst[...] = src[...].astype(dst.dtype)


def _to_bf16(arrays, steps=8):
    specs = [pl.BlockSpec((a.shape[0] // steps, a.shape[1]), lambda i: (i, 0)) for a in arrays]
    return pl.pallas_call(
        _cast_kernel,
        grid=(steps,),
        in_specs=specs,
        out_specs=specs,
        out_shape=[jax.ShapeDtypeStruct(a.shape, BF16) for a in arrays],
        compiler_params=_cparams(("parallel",)),
        name="cast_bf16",
    )(*arrays)


def _ffn_kernel(x_ref, w1_ref, w3_ref, w2_ref, g_ref, b_ref, o_ref):
    o_ref[...] = _swiglu_ln(x_ref[...], w1_ref, w3_ref, w2_ref, g_ref[...], b_ref[...])


def _ffn_ln(x, w1, w3, w2, g, b, tm):
    n = x.shape[0]
    row = pl.BlockSpec((tm, D_MODEL), lambda i: (i, 0))
    return pl.pallas_call(
        _ffn_kernel,
        grid=(n // tm,),
        in_specs=[row, _const_spec(w1.shape), _const_spec(w3.shape), _const_spec(w2.shape),
                  _const_spec(g.shape), _const_spec(b.shape)],
        out_specs=row,
        out_shape=jax.ShapeDtypeStruct((n, D_MODEL), F32),
        compiler_params=_cparams(("parallel",)),
        name="ffn_ln",
    )(x, w1, w3, w2, g, b)


def _mixin_kernel(x_ref, w_ref, wt_ref, wgt_ref, bg_ref, bgt_ref, qg_ref, wqt_ref, kvg_ref, wuk_ref, wuvt_ref,
                  cos_ref, sin_ref, cost_ref, sint_ref,
                  q_ref, kt_ref, v_ref, o_ref, g_ref, gt_ref, aqt_ref, ak_ref, avt_ref, ckv_ref, kr_ref):
    xb = x_ref[...].astype(BF16)
    z = _dot(xb, w_ref[...])
    q_ref[...] = z[:, _C_Q:_C_V].astype(BF16)
    v_ref[...] = z[:, _C_V:_C_O].astype(BF16)
    o_ref[...] = z[:, _C_O:_C_CQ]
    def gates(pre, is_forget):
        logsig = jnp.minimum(pre, 0.0) - jnp.log1p(jnp.exp(-jnp.abs(pre)))
        return jnp.where(is_forget, logsig, pre)
    gc = z[:, _C_G:_C_G + 2 * M_HEADS] + bg_ref[...]
    g_ref[...] = gates(gc, lax.broadcasted_iota(jnp.int32, gc.shape, 1) >= M_HEADS)
    gr = _dot_nt(wgt_ref[...], xb)[:2 * M_HEADS] + bgt_ref[...]
    gt_ref[...] = gates(gr, lax.broadcasted_iota(jnp.int32, gr.shape, 0) >= M_HEADS)
    kt_ref[...] = _dot_nt(wt_ref[...], xb).astype(BF16)
    cq = _rms_norm(z[:, _C_CQ:_C_CKV], qg_ref[...]).astype(BF16)
    qat = _dot_nt(wqt_ref[...], cq)
    ckv = _rms_norm(z[:, _C_CKV:_C_G], kvg_ref[...])
    ckv_ref[...] = ckv
    ckv_b = ckv.astype(BF16)
    kn = _dot(ckv_b, wuk_ref[...])
    vt = _dot_nt(wuvt_ref[...], ckv_b)
    kr = z[:, _C_KRA:_C_KRB] * cos_ref[...] + z[:, _C_KRB:_C_END] * sin_ref[...]
    kr_ref[...] = kr[:, :A_ROPE]
    cost = cost_ref[...]
    sint = sint_ref[...]
    ones_row = (lax.broadcasted_iota(jnp.int32, (V_ROWS, 1), 0) == A_DV).astype(F32)
    nh = A_HEADS * HEAD_TILE
    for h in range(A_HEADS):
        sl = slice(h * HEAD_TILE, (h + 1) * HEAD_TILE)
        sw = slice(nh + h * HEAD_TILE, nh + (h + 1) * HEAD_TILE)
        aqt_ref[h] = ((qat[sl] * cost + qat[sw] * sint) * (ATTN_SCALE * LOG2E)).astype(BF16)
        ak_ref[h] = (kn[:, sl] + kr).astype(BF16)
        avt_ref[h] = (vt[h * V_ROWS:(h + 1) * V_ROWS] + ones_row).astype(BF16)


def _mixer_inputs(x, wts, tables, tm, n_tab_blocks):
    n = x.shape[0]
    cos, sin = tables
    row = lambda w: pl.BlockSpec((tm, w), lambda i: (i, 0))
    colb = lambda r: pl.BlockSpec((r, tm), lambda i: (0, i))
    head = pl.BlockSpec((A_HEADS, tm, HEAD_TILE), lambda i: (0, i, 0))
    head_t = lambda r: pl.BlockSpec((A_HEADS, r, tm), lambda i: (0, 0, i))
    tab = pl.BlockSpec((tm, HEAD_TILE), lambda i: (i % n_tab_blocks, 0))
    tab_t = pl.BlockSpec((HEAD_TILE, tm), lambda i: (0, i % n_tab_blocks))
    consts = [wts["w_main"], wts["w_kt"], wts["w_gt"], wts["bg_row"], wts["bg_col"], wts["q_norm_g"],
              wts["w_q2t"], wts["kv_norm_g"], wts["w_uk_tiles"], wts["w_uvt"]]
    hq = M_HEADS * HEAD_TILE
    hv = M_HEADS * M_DV
    out_shape = (
        jax.ShapeDtypeStruct((n, hq), BF16),
        jax.ShapeDtypeStruct((hq, n), BF16),
        jax.ShapeDtypeStruct((n, hv), BF16),
        jax.ShapeDtypeStruct((n, hv), F32),
        jax.ShapeDtypeStruct((n, 2 * M_HEADS), F32),
        jax.ShapeDtypeStruct((2 * M_HEADS, n), F32),
        jax.ShapeDtypeStruct((A_HEADS, HEAD_TILE, n), BF16),
        jax.ShapeDtypeStruct((A_HEADS, n, HEAD_TILE), BF16),
        jax.ShapeDtypeStruct((A_HEADS, V_ROWS, n), BF16),
        jax.ShapeDtypeStruct((n, KV_LORA), F32),
        jax.ShapeDtypeStruct((n, A_ROPE), F32),
    )
    out_specs = (row(hq), colb(hq), row(hv), row(hv), row(2 * M_HEADS), colb(2 * M_HEADS),
                 head_t(HEAD_TILE), head, head_t(V_ROWS), row(KV_LORA), row(A_ROPE))
    return pl.pallas_call(
        _mixin_kernel,
        grid=(n // tm,),
        in_specs=[row(D_MODEL)] + [_const_spec(c.shape) for c in consts] + [tab, tab, tab_t, tab_t],
        out_specs=out_specs,
        out_shape=out_shape,
        compiler_params=_cparams(("parallel",)),
        name="mixer_inputs",
    )(x, *consts, cos, sin, cos.T, sin.T)


def _split3(x):
    hi = x.astype(BF16)
    r1 = x - hi.astype(F32)
    mid = r1.astype(BF16)
    lo = (r1 - mid.astype(F32)).astype(BF16)
    return hi, mid, lo


def _mlstm_kernel(q_ref, kt_ref, v_ref, g_ref, gt_ref, c0_ref, m0_ref, h_ref, c_ref, m_ref, c_sc, m_sc,
                  *, chunk):
    ci = pl.program_id(1)

    @pl.when(ci == 0)
    def _():
        c_sc[...] = c0_ref[0]
        m_sc[...] = m0_ref[0]

    L = chunk
    row = lax.broadcasted_iota(jnp.int32, (L, L), 0)
    col = lax.broadcasted_iota(jnp.int32, (L, L), 1)
    causal = row >= col
    tril = causal.astype(BF16)
    triu = (row <= col).astype(BF16)
    g = g_ref[...]
    gt = gt_ref[...]
    cum_c = sum(_dot(tril, p) for p in _split3(g))
    cum_r = sum(_dot(p, triu) for p in _split3(gt))
    ones_col = (lax.broadcasted_iota(jnp.int32, (L, LANES), 1) == 0).astype(BF16)
    for h in range(M_HEADS):
        sl = slice(h * HEAD_TILE, (h + 1) * HEAD_TILE)
        qh = q_ref[:, sl]
        kth = kt_ref[sl, :]
        v_aug = jnp.concatenate([v_ref[:, h * M_DV:(h + 1) * M_DV], ones_col], axis=1)
        ig_r = gt[h:h + 1, :]
        lf_r = gt[M_HEADS + h:M_HEADS + h + 1, :]
        cr = cum_r[M_HEADS + h:M_HEADS + h + 1, :]
        cc = cum_c[:, M_HEADS + h:M_HEADS + h + 1]
        m_run = m_sc[h, 0:1, 0:1]
        d = jnp.where(causal, cc - cr + ig_r, -jnp.inf)
        inter = cc + m_run
        m_t = jnp.maximum(inter, jnp.max(d, axis=1, keepdims=True))
        s = (_dot(qh, kth) * jnp.exp(d - m_t)).astype(BF16)
        w_inter = jnp.exp(inter - m_t)
        c_state = c_sc[h]
        r = _dot(s, v_aug) + w_inter * _dot(qh, c_state.astype(BF16))
        qn = r[:, M_DV:M_DV + 1]
        den = jnp.maximum(jnp.abs(qn), jnp.exp(-m_t))
        h_ref[:, h * M_DV:(h + 1) * M_DV] = r[:, :M_DV] / den
        last = jnp.sum(lf_r, axis=1, keepdims=True)
        g_r = last - cr + ig_r
        m_new = jnp.maximum(last + m_run, jnp.max(g_r, axis=1, keepdims=True))
        decay = jnp.exp(last + m_run - m_new)
        wkt = (kth.astype(F32) * jnp.exp(g_r - m_new)).astype(BF16)
        c_sc[h] = decay * c_state + _dot(wkt, v_aug)
        m_sc[h] = jnp.broadcast_to(m_new, (8, LANES))

    @pl.when(ci == pl.num_programs(1) - 1)
    def _():
        c_ref[0] = c_sc[...]
        m_ref[0] = m_sc[...]


def _mlstm_scan(q, kt, v, g, gt, c0, m0, batch, seq, chunk):
    nc = seq // chunk
    hq = M_HEADS * HEAD_TILE
    hv = M_HEADS * M_DV
    row = lambda w: pl.BlockSpec((chunk, w), lambda b, c: (b * nc + c, 0))
    colb = lambda r: pl.BlockSpec((r, chunk), lambda b, c: (0, b * nc + c))
    st_in = lambda s: pl.BlockSpec((1,) + s, lambda b, c: (0,) * (len(s) + 1))
    st_out = lambda s: pl.BlockSpec((1,) + s, lambda b, c: (b,) + (0,) * len(s))
    cs, ms = (M_HEADS, HEAD_TILE, 2 * LANES), (M_HEADS, 8, LANES)
    return pl.pallas_call(
        functools.partial(_mlstm_kernel, chunk=chunk),
        grid=(batch, nc),
        in_specs=[row(hq), colb(hq), row(hv), row(2 * M_HEADS), colb(2 * M_HEADS), st_in(cs), st_in(ms)],
        out_specs=(row(hv), st_out(cs), st_out(ms)),
        out_shape=(jax.ShapeDtypeStruct((batch * seq, hv), F32),
                   jax.ShapeDtypeStruct((batch,) + cs, F32),
                   jax.ShapeDtypeStruct((batch,) + ms, F32)),
        scratch_shapes=[pltpu.VMEM(cs, F32), pltpu.VMEM(ms, F32)],
        compiler_params=_cparams(("parallel", "arbitrary")),
        name="mlstm_scan",
    )(q, kt, v, g, gt, c0, m0)


def _mlstm_step_kernel(q_ref, k_ref, kt_ref, v_ref, gc_ref, gr_ref, c0_ref, n0_ref, m0_ref,
                       h_ref, c_ref, n_ref, m_ref):
    T = q_ref.shape[1]
    row = lax.broadcasted_iota(jnp.int32, (T, T), 0)
    col = lax.broadcasted_iota(jnp.int32, (T, T), 1)
    causal = (row >= col)[None]
    q = q_ref[...]
    k = k_ref[...]
    v = v_ref[...]
    ig_c, lf_c = gc_ref[:, :, 0:1], gc_ref[:, :, 1:2]
    ig_r, lf_r = gr_ref[:, 0:1, :], gr_ref[:, 1:2, :]
    m0 = m0_ref[...]
    c0 = c0_ref[...]
    n0 = n0_ref[...]
    cum_c = jnp.sum(jnp.where(causal, lf_r, 0.0), axis=2, keepdims=True)
    cum_r = jnp.sum(jnp.where((row <= col)[None], lf_c, 0.0), axis=1, keepdims=True)
    d = jnp.where(causal, cum_c - cum_r + ig_r, -jnp.inf)
    inter = cum_c + m0
    m_t = jnp.maximum(inter, jnp.max(d, axis=2, keepdims=True))
    qk = jnp.einsum("gtd,gsd->gts", q, k, preferred_element_type=F32)
    s = qk * jnp.exp(d - m_t)
    w_inter = jnp.exp(inter - m_t)
    num = (jnp.einsum("gts,gsv->gtv", s.astype(BF16), v, preferred_element_type=F32)
           + w_inter * jnp.einsum("gtd,gdv->gtv", q, c0.astype(BF16), preferred_element_type=F32))
    qn = jnp.sum(s, axis=2, keepdims=True) + w_inter * jnp.sum(q.astype(F32) * n0, axis=2, keepdims=True)
    h_ref[...] = num / jnp.maximum(jnp.abs(qn), jnp.exp(-m_t))
    last = cum_c[:, T - 1:T, :]
    g_c = last - cum_c + ig_c
    g_r = last - cum_r + ig_r
    m_new = jnp.maximum(last + m0, jnp.max(g_c, axis=1, keepdims=True))
    decay = jnp.exp(last + m0 - m_new)
    wk = k.astype(F32) * jnp.exp(g_c - m_new)
    wkt = (kt_ref[...].astype(F32) * jnp.exp(g_r - m_new)).astype(BF16)
    c_ref[...] = decay * c0 + jnp.einsum("gds,gsv->gdv", wkt, v, preferred_element_type=F32)
    n_ref[...] = decay * n0 + jnp.sum(wk, axis=1, keepdims=True)
    m_ref[...] = m_new


def _mlstm_step(q, k, kt, v, gc, gr, c0, n0, m0, gb):
    gtot, T = q.shape[0], q.shape[1]
    blk = lambda a, b: pl.BlockSpec((gb, a, b), lambda i: (i, 0, 0))
    return pl.pallas_call(
        _mlstm_step_kernel,
        grid=(gtot // gb,),
        in_specs=[blk(T, M_DQK), blk(T, M_DQK), blk(M_DQK, T), blk(T, M_DV), blk(T, 2), blk(2, T),
                  blk(M_DQK, M_DV), blk(1, M_DQK), blk(1, 1)],
        out_specs=(blk(T, M_DV), blk(M_DQK, M_DV), blk(1, M_DQK), blk(1, 1)),
        out_shape=(jax.ShapeDtypeStruct((gtot, T, M_DV), F32),
                   jax.ShapeDtypeStruct((gtot, M_DQK, M_DV), F32),
                   jax.ShapeDtypeStruct((gtot, 1, M_DQK), F32),
                   jax.ShapeDtypeStruct((gtot, 1, 1), F32)),
        compiler_params=_cparams(("parallel",)),
        name="mlstm_step",
    )(q, k, kt, v, gc, gr, c0, n0, m0)


def _attn_kernel(qi_tab, ki_tab, qt_ref, k_ref, vt_ref, km_ref, vmt_ref, o_ref, acc_sc, m_sc):
    step = pl.program_id(1)
    qi = qi_tab[step]
    ki = ki_tab[step]
    tq = qt_ref.shape[2]

    @pl.when(ki == 0)
    def _():
        ss = [_dot(km_ref[h], qt_ref[h]) for h in range(A_HEADS)]
        ms = [jnp.max(s, axis=0, keepdims=True) for s in ss]
        for h in range(A_HEADS):
            acc_sc[h] = _dot(vmt_ref[h], jnp.exp2(ss[h] - ms[h]).astype(BF16))
            m_sc[h] = ms[h]

    @pl.when(ki < qi)
    def _():
        s_next = _dot(k_ref[0], qt_ref[0])
        for h in range(A_HEADS):
            s = s_next
            if h + 1 < A_HEADS:
                s_next = _dot(k_ref[h + 1], qt_ref[h + 1])
            m_prev = m_sc[h]
            m_new = jnp.maximum(m_prev, jnp.max(s, axis=0, keepdims=True))
            p = jnp.exp2(s - m_new).astype(BF16)
            acc_sc[h] = jnp.exp2(m_prev - m_new) * acc_sc[h] + _dot(vt_ref[h], p)
            m_sc[h] = m_new

    def update_diagonal():
        hk = tq // 2
        row = lax.broadcasted_iota(jnp.int32, (hk, tq), 0)
        col = lax.broadcasted_iota(jnp.int32, (hk, tq), 1)
        keep_a = row <= col
        keep_b = keep_a[:, :hk]
        scores_a = lambda h: _dot(k_ref[h, :hk], qt_ref[h])
        s_next = scores_a(0)
        for h in range(A_HEADS):
            s1 = jnp.where(keep_a, s_next, -jnp.inf)
            s2 = jnp.where(keep_b, _dot(k_ref[h, hk:], qt_ref[h, :, hk:]), -jnp.inf)
            if h + 1 < A_HEADS:
                s_next = scores_a(h + 1)
            m_prev = m_sc[h]
            m1 = jnp.maximum(m_prev, jnp.max(s1, axis=0, keepdims=True))
            m2 = jnp.maximum(m1[:, hk:], jnp.max(s2, axis=0, keepdims=True))
            m_new = jnp.concatenate([m1[:, :hk], m2], axis=1)
            p1 = jnp.exp2(s1 - m_new).astype(BF16)
            p2 = jnp.exp2(s2 - m2).astype(BF16)
            pv = _dot(vt_ref[h, :, :hk], p1)
            pv = pv + jnp.concatenate([jnp.zeros((V_ROWS, hk), F32), _dot(vt_ref[h, :, hk:], p2)], axis=1)
            acc_sc[h] = jnp.exp2(m_prev - m_new) * acc_sc[h] + pv
            m_sc[h] = m_new

    @pl.when(ki == qi)
    def _():
        update_diagonal()
        for j in range(A_HEADS // 2):
            ae = acc_sc[2 * j]
            ao = acc_sc[2 * j + 1]
            pair = jnp.concatenate([ae[:A_DV] / ae[A_DV:A_DV + 1], ao[:A_DV] / ao[A_DV:A_DV + 1]], axis=0)
            o_ref[:, j * HEAD_TILE:(j + 1) * HEAD_TILE] = pair.T.astype(o_ref.dtype)


def _attention_prompt(aqt, ak, avt, k_meta, vt_meta, batch, seq):
    blk = ATTN_BLOCK
    nq = seq // blk
    qi_np, ki_np = zip(*[(i, j) for i in range(nq) for j in range(i + 1)])
    qi_tab = jnp.asarray(np.array(qi_np, np.int32))
    ki_tab = jnp.asarray(np.array(ki_np, np.int32))
    rows = lambda tab: pl.BlockSpec((A_HEADS, blk, HEAD_TILE),
                                    lambda b, s, qt, kt: (0, b * nq + (qt, kt)[tab][s], 0))
    cols = lambda tab, r: pl.BlockSpec((A_HEADS, r, blk),
                                       lambda b, s, qt, kt: (0, 0, b * nq + (qt, kt)[tab][s]))
    grid_spec = pltpu.PrefetchScalarGridSpec(
        num_scalar_prefetch=2,
        grid=(batch, len(qi_np)),
        in_specs=[cols(0, HEAD_TILE), rows(1), cols(1, V_ROWS),
                  pl.BlockSpec((A_HEADS, N_META, HEAD_TILE), lambda b, s, qt, kt: (0, 0, 0)),
                  pl.BlockSpec((A_HEADS, V_ROWS, N_META), lambda b, s, qt, kt: (0, 0, 0))],
        out_specs=pl.BlockSpec((blk, A_HEADS * A_DV), lambda b, s, qt, kt: (b * nq + qt[s], 0)),
        scratch_shapes=[pltpu.VMEM((A_HEADS, V_ROWS, blk), F32), pltpu.VMEM((A_HEADS, 1, blk), F32)],
    )
    return pl.pallas_call(
        _attn_kernel,
        grid_spec=grid_spec,
        out_shape=jax.ShapeDtypeStruct((batch * seq, A_HEADS * A_DV), BF16),
        compiler_params=_cparams(("parallel", "arbitrary")),
        name="attn_prompt",
    )(qi_tab, ki_tab, aqt, ak, avt, k_meta, vt_meta)


def _absorb_kernel(qt_ref, wuk_ref, o_ref):
    for h in range(A_HEADS):
        o_ref[h] = _dot(wuk_ref[h], qt_ref[h]).astype(BF16)


def _absorb_q(aqt, wuk_tiles):
    n = aqt.shape[2]
    return pl.pallas_call(
        _absorb_kernel,
        out_shape=jax.ShapeDtypeStruct((A_HEADS, KV_LORA, n), BF16),
        compiler_params=pltpu.CompilerParams(vmem_limit_bytes=VMEM_LIMIT),
        name="absorb_q",
    )(aqt, wuk_tiles)


def _decode_attn_kernel(pt_ref, ql_ref, qr_ref, cn_ref, kn_ref, lat_hbm, krt_hbm, o_ref,
                        lat_buf, krt_buf, sem_lat, sem_krt):
    g = pl.program_id(0)
    ng = pl.num_programs(0)
    sps, sub, chunk, n_slots = DEC_SEQS, DEC_SUB, DEC_CHUNK, DEC_SLOTS
    n_sub_seq = pt_ref.shape[1] // sub
    spc = chunk // sub
    n_sub = sps * n_sub_seq
    ahead = 2 * spc
    assert n_sub_seq % spc == 0 and (n_sub // spc) % n_slots == 0 and n_slots >= 4

    def copies(step, j):
        seq = step * sps + j // n_sub_seq
        slot = (j // spc) % n_slots
        for k in range(sub):
            page = pt_ref[seq, (j % n_sub_seq) * sub + k]
            pos = (j % spc) * sub + k
            yield pltpu.make_async_copy(lat_hbm.at[page], lat_buf.at[slot, pos], sem_lat.at[slot])
            yield pltpu.make_async_copy(krt_hbm.at[page], krt_buf.at[slot, pos], sem_krt.at[slot])

    def start(step, j):
        for cp in copies(step, j):
            cp.start()

    def wait_chunk(c):
        for j in range(c * spc, (c + 1) * spc):
            for cp in copies(g, j):
                cp.wait()

    @pl.when(g == 0)
    def _():
        for j in range(ahead):
            start(0, j)

    rows = ql_ref.shape[1]

    def scores(j):
        q = j // n_sub_seq
        slot = (j // spc) % n_slots
        ks = range((j % spc) * sub, (j % spc + 1) * sub)
        lats = [lat_buf[slot, k].astype(BF16) for k in ks]
        s = jnp.concatenate(
            [_dot_nt(ql_ref[q], lat) + _dot(qr_ref[q], krt_buf[slot, k].astype(BF16)) for k, lat in zip(ks, lats)],
            axis=1)
        return lats, s

    def finish(q, m_run, l_run, acc):
        T = cn_ref.shape[1]
        cn = cn_ref[q].astype(BF16).astype(F32)
        kn = kn_ref[q].astype(BF16).astype(F32)
        qlf = ql_ref[q].astype(F32)
        qrf = qr_ref[q].astype(F32)
        t_of_row = lax.broadcasted_iota(jnp.int32, (rows, 1), 0) % T
        s_new = []
        for t in range(T):
            st = (jnp.sum(qlf * cn[t:t + 1], axis=1, keepdims=True)
                  + jnp.sum(qrf * kn[t:t + 1], axis=1, keepdims=True))
            s_new.append(jnp.where(t_of_row >= t, st, -jnp.inf))
        m_fin = m_run
        for st in s_new:
            m_fin = jnp.maximum(m_fin, st)
        alpha = jnp.exp2(m_run - m_fin)
        acc = alpha * acc
        l_run = alpha * l_run
        for t, st in enumerate(s_new):
            pt = jnp.exp2(st - m_fin)
            l_run = l_run + pt
            acc = acc + pt * cn[t:t + 1]
        o_ref[q] = acc / l_run

    wait_chunk(0)
    nxt = scores(0)
    for j in range(n_sub):
        if j % n_sub_seq == 0:
            m_run = jnp.full((rows, 1), -jnp.inf, F32)
            l_run = jnp.zeros((rows, 1), F32)
            acc = jnp.zeros((rows, KV_LORA), F32)
        lats, s = nxt
        if j + 1 < n_sub:
            if (j + 1) % spc == 0:
                wait_chunk((j + 1) // spc)
            nxt = scores(j + 1)
        jj = j + ahead
        if jj < n_sub:
            start(g, jj)
        else:
            @pl.when(g + 1 < ng)
            def _():
                start(g + 1, jj - n_sub)
        m_new = jnp.maximum(m_run, jnp.max(s, axis=1, keepdims=True))
        p = jnp.exp2(s - m_new)
        alpha = jnp.exp2(m_run - m_new)
        l_run = alpha * l_run + jnp.sum(p, axis=1, keepdims=True)
        pb = p.astype(BF16)
        pv = sum(_dot(pb[:, k * PAGE_SIZE:(k + 1) * PAGE_SIZE], lat) for k, lat in enumerate(lats))
        acc = alpha * acc + pv
        m_run = m_new
        if j % n_sub_seq == n_sub_seq - 1:
            finish(j // n_sub_seq, m_run, l_run, acc)


def _decode_attention(page_table, q_lat, q_rope, c_new, kr_new, cache_lat, cache_krt):
    nb = page_table.shape[0]
    rows = q_lat.shape[1]
    per_b = lambda a, b: pl.BlockSpec((DEC_SEQS, a, b), lambda i, pt: (i, 0, 0))
    hbm = pl.BlockSpec(memory_space=pl.ANY)
    grid_spec = pltpu.PrefetchScalarGridSpec(
        num_scalar_prefetch=1,
        grid=(nb // DEC_SEQS,),
        in_specs=[per_b(rows, KV_LORA), per_b(rows, A_ROPE), per_b(c_new.shape[1], KV_LORA),
                  per_b(kr_new.shape[1], A_ROPE), hbm, hbm],
        out_specs=per_b(rows, KV_LORA),
        scratch_shapes=[pltpu.VMEM((DEC_SLOTS, DEC_CHUNK, PAGE_SIZE, KV_LORA), F32),
                        pltpu.VMEM((DEC_SLOTS, DEC_CHUNK, A_ROPE, PAGE_SIZE), F32),
                        pltpu.SemaphoreType.DMA((DEC_SLOTS,)), pltpu.SemaphoreType.DMA((DEC_SLOTS,))],
    )
    return pl.pallas_call(
        _decode_attn_kernel,
        grid_spec=grid_spec,
        out_shape=jax.ShapeDtypeStruct((nb, rows, KV_LORA), F32),
        compiler_params=_cparams(("arbitrary",)),
        name="attn_decode",
    )(page_table, q_lat, q_rope, c_new, kr_new, cache_lat, cache_krt)


def _lat2attn_kernel(o_ref, wuv_ref, a_ref):
    for j in range(A_HEADS // 2):
        pair = (_dot(o_ref[2 * j].astype(BF16), wuv_ref[2 * j])
                + _dot(o_ref[2 * j + 1].astype(BF16), wuv_ref[2 * j + 1]))
        a_ref[:, j * HEAD_TILE:(j + 1) * HEAD_TILE] = pair.astype(a_ref.dtype)


def _lat2attn(o_lat, wuv_tiles):
    n = o_lat.shape[1]
    return pl.pallas_call(
        _lat2attn_kernel,
        out_shape=jax.ShapeDtypeStruct((n, A_HEADS * A_DV), BF16),
        compiler_params=pltpu.CompilerParams(vmem_limit_bytes=VMEM_LIMIT),
        name="lat2attn",
    )(o_lat, wuv_tiles)


def _mixout_kernel(x_ref, hm_ref, om_ref, at_ref, ng_ref, wo_ref, g2_ref, b2_ref,
                   w1_ref, w3_ref, w2_ref, g3_ref, b3_ref, o_ref):
    x = x_ref[...]
    parts = []
    for h in range(M_HEADS):
        sl = slice(h * M_DV, (h + 1) * M_DV)
        hn = _rms_norm(hm_ref[:, sl], ng_ref[:, sl])
        parts.append((hn * jax.nn.sigmoid(om_ref[:, sl])).astype(BF16))
    cat = jnp.concatenate(parts + [at_ref[...]], axis=1)
    x2 = _layer_norm(ALPHA * x + _dot(cat, wo_ref[...]), g2_ref[...], b2_ref[...])
    o_ref[...] = _swiglu_ln(x2, w1_ref, w3_ref, w2_ref, g3_ref[...], b3_ref[...])


def _mixer_output_ffn(x, hm, om, attn, wts, tm):
    n = x.shape[0]
    row = lambda w: pl.BlockSpec((tm, w), lambda i: (i, 0))
    consts = [wts["mlstm_norm_g"], wts["w_out"], wts["ln2_g"], wts["ln2_b"],
              wts["ffn2_w1"], wts["ffn2_w3"], wts["ffn2_w2"], wts["ln3_g"], wts["ln3_b"]]
    hv = M_HEADS * M_DV
    return pl.pallas_call(
        _mixout_kernel,
        grid=(n // tm,),
        in_specs=[row(D_MODEL), row(hv), row(hv), row(A_HEADS * A_DV)] + [_const_spec(c.shape) for c in consts],
        out_specs=row(D_MODEL),
        out_shape=jax.ShapeDtypeStruct((n, D_MODEL), F32),
        compiler_params=_cparams(("parallel",)),
        name="mixer_output_ffn",
    )(x, hm, om, attn, *consts)


def _pack_weights(w_in, b_gates, mlstm_norm_g, q_norm_g, w_uq, kv_norm_g, w_uk, w_uv, w_out):
    o_q, o_k, o_v, o_o = 0, M_HEADS * M_DQK, 2 * M_HEADS * M_DQK, 2 * M_HEADS * M_DQK + M_HEADS * M_DV
    o_g = o_o + M_HEADS * M_DV
    o_cq = o_g + 2 * M_HEADS
    o_ckv = o_cq + Q_LORA
    o_kr = o_ckv + KV_LORA
    half = A_ROPE // 2
    dq = A_NOPE + A_ROPE

    def pad_axis(a, axis, lo, size):
        pads = [(0, 0)] * a.ndim
        pads[axis] = (lo, size - lo - a.shape[axis])
        return jnp.pad(a, pads)

    q_cols = pad_axis(w_in[:, o_q:o_k].reshape(D_MODEL, M_HEADS, M_DQK), 2, 0, HEAD_TILE).reshape(D_MODEL, -1)
    kr = w_in[:, o_kr:o_kr + A_ROPE]
    kr_sw = jnp.concatenate([kr[:, half:], kr[:, :half]], 1)
    w_main = jnp.concatenate(
        [q_cols, w_in[:, o_v:o_g], w_in[:, o_cq:o_kr], pad_axis(w_in[:, o_g:o_cq], 1, 0, LANES),
         pad_axis(kr, 1, 0, LANES), pad_axis(kr_sw, 1, 0, LANES)], 1)
    assert w_main.shape[1] == _C_END
    w_kt = pad_axis((w_in[:, o_k:o_v].T * (M_DQK ** -0.5)).reshape(M_HEADS, M_DQK, D_MODEL), 1, 0, HEAD_TILE)
    w_kt = w_kt.reshape(M_HEADS * HEAD_TILE, D_MODEL)
    w_gt = pad_axis(w_in[:, o_g:o_cq].T, 0, 0, 16)
    uq = w_uq.reshape(Q_LORA, A_HEADS, dq)
    q_tiles = pad_axis(jnp.concatenate([uq[..., A_NOPE:], uq[..., :A_NOPE]], -1), 2, 0, HEAD_TILE)
    q_sw_tiles = pad_axis(jnp.concatenate([uq[..., A_NOPE + half:], uq[..., A_NOPE:A_NOPE + half]], -1),
                          2, 0, HEAD_TILE)
    w_q2 = jnp.concatenate([q_tiles.reshape(Q_LORA, -1), q_sw_tiles.reshape(Q_LORA, -1)], 1)
    uk_tiles = pad_axis(w_uk, 2, A_ROPE, HEAD_TILE)
    w_uvt = pad_axis(w_uv.transpose(1, 2, 0), 1, 0, V_ROWS).reshape(A_HEADS * V_ROWS, KV_LORA)
    uv_pairs = w_uv.reshape(KV_LORA, A_HEADS // 2, 2, A_DV)
    uv_heads = jnp.stack([pad_axis(uv_pairs[:, :, 0], 2, 0, HEAD_TILE),
                          pad_axis(uv_pairs[:, :, 1], 2, A_DV, HEAD_TILE)], 2)
    bf = lambda a: a.astype(BF16)
    return {
        "w_main": bf(w_main), "w_kt": bf(w_kt), "w_gt": bf(w_gt),
        "bg_row": b_gates.reshape(1, -1), "bg_col": b_gates.reshape(-1, 1),
        "q_norm_g": q_norm_g.reshape(1, -1), "kv_norm_g": kv_norm_g.reshape(1, -1),
        "w_q2t": bf(w_q2.T),
        "w_uk_tiles": bf(uk_tiles.reshape(KV_LORA, -1)),
        "w_uvt": bf(w_uvt),
        "w_uk_heads": bf(uk_tiles.transpose(1, 0, 2)),
        "w_uv_heads": bf(uv_heads.reshape(KV_LORA, A_HEADS, HEAD_TILE).transpose(1, 0, 2)),
        "mlstm_norm_g": mlstm_norm_g.reshape(1, -1), "w_out": bf(w_out),
    }


def _rope_tables(pos):
    half = A_ROPE // 2
    inv = ROPE_BASE ** (-np.arange(half, dtype=np.float64) / half)
    ang = np.asarray(pos, np.float64)[:, None] * inv[None, :]
    cos, sin = np.cos(ang), np.sin(ang)
    n = ang.shape[0]
    cos_t = np.concatenate([cos, cos, np.ones((n, A_NOPE)), np.zeros((n, HEAD_TILE - A_NOPE - A_ROPE))], 1)
    sin_t = np.concatenate([-sin, sin, np.zeros((n, HEAD_TILE - A_ROPE))], 1)
    return cos_t.astype(np.float32), sin_t.astype(np.float32)


def kernel(x_prompt, x_sample, cache_kv_latent, cache_k_rope, state_mlstm_C, state_mlstm_n, state_mlstm_m, page_table, meta_tokens, ffn1_w1, ffn1_w3, ffn1_w2, ln1_g, ln1_b, w_in, b_gates, mlstm_norm_g, q_norm_g, w_uq, kv_norm_g, w_uk, w_uv, w_out, ln2_g, ln2_b, ffn2_w1, ffn2_w3, ffn2_w2, ln3_g, ln3_b):
    assert w_in.shape[0] == DEPTH == 1
    B, S, _ = x_prompt.shape
    Bd, Td, _ = x_sample.shape
    n_real, n_dec = B * S, Bd * Td
    l = 0
    vec = lambda a: a[l].reshape(1, -1)
    wts = _pack_weights(w_in[l], b_gates[l], mlstm_norm_g[l], q_norm_g[l], w_uq[l], kv_norm_g[l],
                        w_uk[l], w_uv[l], w_out[l])
    ffn_bf = _to_bf16([ffn1_w1[l], ffn1_w3[l], ffn1_w2[l], ffn2_w1[l], ffn2_w3[l], ffn2_w2[l]])
    wts.update({"ln2_g": vec(ln2_g), "ln2_b": vec(ln2_b), "ln3_g": vec(ln3_g), "ln3_b": vec(ln3_b),
                "ffn2_w1": ffn_bf[3], "ffn2_w3": ffn_bf[4], "ffn2_w2": ffn_bf[5]})
    f1 = (ffn_bf[0], ffn_bf[1], ffn_bf[2], vec(ln1_g), vec(ln1_b))

    x_real = x_prompt.reshape(n_real, D_MODEL)
    x_extra = jnp.concatenate([meta_tokens.astype(F32), x_sample.reshape(n_dec, D_MODEL)], 0)
    n_extra = x_extra.shape[0]
    tm = 512

    h_real = _ffn_ln(x_real, *f1, tm)
    h_extra = _ffn_ln(x_extra, *f1, n_extra)

    pos_e = np.concatenate([np.arange(N_META), np.tile(page_table.shape[1] * PAGE_SIZE + np.arange(Td), Bd)])
    (q_r, kt_r, v_r, om_r, g_r, gt_r, aqt_r, ak_r, avt_r, ckv_r, kr_r) = _mixer_inputs(
        h_real, wts, _rope_tables(N_META + np.arange(S)), tm, S // tm)
    (q_e, kt_e, v_e, om_e, g_e, gt_e, aqt_e, ak_e, avt_e, ckv_e, kr_e) = _mixer_inputs(
        h_extra, wts, _rope_tables(pos_e), n_extra, 1)

    cs, ms = (1, M_HEADS, HEAD_TILE, 2 * LANES), (1, M_HEADS, 8, LANES)
    _, c_meta, m_meta = _mlstm_scan(q_e[:N_META], kt_e[:, :N_META], v_e[:N_META], g_e[:N_META],
                                    gt_e[:, :N_META], jnp.zeros(cs, F32), jnp.zeros(ms, F32), 1, N_META, N_META)
    hm_r, c_p, m_p = _mlstm_scan(q_r, kt_r, v_r, g_r, gt_r, c_meta, m_meta, B, S, MLSTM_CHUNK)
    mlstm_c_p = c_p[:, :, :M_DQK, :M_DV][None]
    mlstm_n_p = c_p[:, :, :M_DQK, M_DV][None]
    mlstm_m_p = m_p[:, :, 0, 0][None]

    G = Bd * M_HEADS
    def dec_heads(a, w, used):
        return a.reshape(Bd, Td, M_HEADS, w)[..., :used].transpose(0, 2, 1, 3).reshape(G, Td, used)
    q_d = dec_heads(q_e[N_META:], HEAD_TILE, M_DQK)
    v_d = dec_heads(v_e[N_META:], M_DV, M_DV)
    kt_d = (kt_e[:, N_META:].reshape(M_HEADS, HEAD_TILE, Bd, Td)[:, :M_DQK]
            .transpose(2, 0, 1, 3).reshape(G, M_DQK, Td))
    k_d = kt_d.transpose(0, 2, 1)
    gdec = g_e[N_META:].reshape(Bd, Td, 2, M_HEADS).transpose(0, 3, 1, 2).reshape(G, Td, 2)
    hm_d, c_d, n_d, m_d = _mlstm_step(
        q_d, k_d, kt_d, v_d, gdec, gdec.transpose(0, 2, 1),
        state_mlstm_C[l].reshape(G, M_DQK, M_DV), state_mlstm_n[l].reshape(G, 1, M_DQK),
        state_mlstm_m[l].reshape(G, 1, 1), 64)
    hm_d = hm_d.reshape(Bd, M_HEADS, Td, M_DV).transpose(0, 2, 1, 3).reshape(n_dec, M_HEADS * M_DV)

    attn_r = _attention_prompt(aqt_r, ak_r, avt_r, ak_e[:, :N_META], avt_e[:, :, :N_META], B, S)

    aqt_d = aqt_e[:, :, N_META:]
    q_lat_t = _absorb_q(aqt_d, wts["w_uk_heads"])
    rows = A_HEADS * Td
    by_seq = lambda a: (a.reshape(A_HEADS, a.shape[1], Bd, Td).transpose(2, 0, 3, 1)
                        .reshape(Bd, rows, a.shape[1]))
    o_lat = _decode_attention(
        page_table, by_seq(q_lat_t), by_seq(aqt_d[:, :A_ROPE]),
        ckv_e[N_META:].reshape(Bd, Td, KV_LORA), kr_e[N_META:].reshape(Bd, Td, A_ROPE),
        cache_kv_latent[l], jnp.swapaxes(cache_k_rope[l], 1, 2))
    o_lat = o_lat.reshape(Bd, A_HEADS, Td, KV_LORA).transpose(1, 0, 2, 3).reshape(A_HEADS, n_dec, KV_LORA)
    attn_d = _lat2attn(o_lat, wts["w_uv_heads"])

    y_real = _mixer_output_ffn(h_real, hm_r, om_r, attn_r, wts, tm)
    y_dec = _mixer_output_ffn(h_extra[N_META:], hm_d, om_e[N_META:], attn_d, wts, n_dec)

    bc = lambda a: jnp.broadcast_to(a[None], (B,) + a.shape)
    kv_p = jnp.concatenate([bc(ckv_e[:N_META]), ckv_r.reshape(B, S, KV_LORA)], 1)[None]
    kr_p = jnp.concatenate([bc(kr_e[:N_META]), kr_r.reshape(B, S, A_ROPE)], 1)[None]
    return (y_real.reshape(B, S, D_MODEL), y_dec.reshape(Bd, Td, D_MODEL), kv_p, kr_p,
            mlstm_c_p, mlstm_n_p, mlstm_m_p,
            ckv_e[N_META:].reshape(1, Bd, Td, KV_LORA), kr_e[N_META:].reshape(1, Bd, Td, A_ROPE),
            c_d.reshape(1, Bd, M_HEADS, M_DQK, M_DV), n_d.reshape(1, Bd, M_HEADS, M_DQK),
            m_d.reshape(1, Bd, M_HEADS))
```

```python
import functools
import math

import numpy as np
import jax
import jax.numpy as jnp
from jax import lax
from jax.experimental import pallas as pl
from jax.experimental.pallas import tpu as pltpu

F32 = jnp.float32
BF16 = jnp.bfloat16

D_MODEL = 1024
N_META = 16
M_HEADS = 4
M_DQK = 64
M_DV = 128
A_HEADS = 8
A_NOPE = 64
A_ROPE = 32
A_DV = 64
Q_LORA = 256
KV_LORA = 256
ROPE_BASE = 10000.0
PAGE_SIZE = 128
D_FF = 2816
DEPTH = 1
ALPHA = (2.0 * DEPTH) ** 0.25
LN_EPS = 1e-5
RMS_EPS = 1e-6
ATTN_SCALE = (A_NOPE + A_ROPE) ** -0.5
LOG2E = math.log2(math.e)

LANES = 128
HEAD_TILE = LANES
FF_CHUNK = 256
MLSTM_CHUNK = 256
ATTN_BLOCK = 1024
V_ROWS = 80
DEC_SEQS = 4
DEC_SUB = 8
DEC_CHUNK = 64
DEC_SLOTS = 4
VMEM_LIMIT = 56 * 1024 * 1024

_C_Q = 0
_C_V = _C_Q + M_HEADS * HEAD_TILE
_C_O = _C_V + M_HEADS * M_DV
_C_CQ = _C_O + M_HEADS * M_DV
_C_CKV = _C_CQ + Q_LORA
_C_G = _C_CKV + KV_LORA
_C_KRA = _C_G + LANES
_C_KRB = _C_KRA + LANES
_C_END = _C_KRB + LANES


def _cparams(sem):
    return pltpu.CompilerParams(dimension_semantics=sem, vmem_limit_bytes=VMEM_LIMIT)


def _const_spec(shape):
    nd = len(shape)
    return pl.BlockSpec(shape, lambda *_: (0,) * nd, pipeline_mode=pl.Buffered(1))


def _layer_norm(r, g, b):
    mu = jnp.mean(r, axis=-1, keepdims=True)
    d = r - mu
    var = jnp.mean(d * d, axis=-1, keepdims=True)
    return d * lax.rsqrt(var + LN_EPS) * g + b


def _rms_norm(x, g):
    return x * lax.rsqrt(jnp.mean(x * x, axis=-1, keepdims=True) + RMS_EPS) * g


def _dot(a, b):
    return jnp.dot(a, b, preferred_element_type=F32)


def _dot_nt(a, b):
    return lax.dot_general(a, b, (((1,), (1,)), ((), ())), preferred_element_type=F32)


def _swiglu_ln(x, w1_ref, w3_ref, w2_ref, g, b):
    xb = x.astype(BF16)
    y = None
    for j in range(D_FF // FF_CHUNK):
        sl = slice(j * FF_CHUNK, (j + 1) * FF_CHUNK)
        a = _dot(xb, w1_ref[:, sl])
        c = _dot(xb, w3_ref[:, sl])
        hj = (a * jax.nn.sigmoid(a) * c).astype(BF16)
        yj = _dot(hj, w2_ref[sl, :])
        y = yj if y is None else y + yj
    return _layer_norm(ALPHA * x + 0.5 * y, g, b)


def _cast_kernel(*refs):
    n = len(refs) // 2
    for src, dst in zip(refs[:n], refs[n:]):
        dst[...] = src[...].astype(dst.dtype)


def _to_bf16(arrays, steps=8):
    specs = [pl.BlockSpec((a.shape[0] // steps, a.shape[1]), lambda i: (i, 0)) for a in arrays]
    return pl.pallas_call(
        _cast_kernel,
        grid=(steps,),
        in_specs=specs,
        out_specs=specs,
        out_shape=[jax.ShapeDtypeStruct(a.shape, BF16) for a in arrays],
        compiler_params=_cparams(("parallel",)),
        name="cast_bf16",
    )(*arrays)


def _ffn_kernel(x_ref, w1_ref, w3_ref, w2_ref, g_ref, b_ref, o_ref):
    o_ref[...] = _swiglu_ln(x_ref[...], w1_ref, w3_ref, w2_ref, g_ref[...], b_ref[...])


def _ffn_ln(x, w1, w3, w2, g, b, tm):
    n = x.shape[0]
    row = pl.BlockSpec((tm, D_MODEL), lambda i: (i, 0))
    return pl.pallas_call(
        _ffn_kernel,
        grid=(n // tm,),
        in_specs=[row, _const_spec(w1.shape), _const_spec(w3.shape), _const_spec(w2.shape),
                  _const_spec(g.shape), _const_spec(b.shape)],
        out_specs=row,
        out_shape=jax.ShapeDtypeStruct((n, D_MODEL), F32),
        compiler_params=_cparams(("parallel",)),
        name="ffn_ln",
    )(x, w1, w3, w2, g, b)


def _mixin_kernel(x_ref, w_ref, wt_ref, wgt_ref, bg_ref, bgt_ref, qg_ref, wqt_ref, kvg_ref, wuk_ref, wuvt_ref,
                  cos_ref, sin_ref, cost_ref, sint_ref,
                  q_ref, kt_ref, v_ref, o_ref, g_ref, gt_ref, aqt_ref, ak_ref, avt_ref, ckv_ref, kr_ref):
    xb = x_ref[...].astype(BF16)
    z = _dot(xb, w_ref[...])
    q_ref[...] = z[:, _C_Q:_C_V].astype(BF16)
    v_ref[...] = z[:, _C_V:_C_O].astype(BF16)
    o_ref[...] = z[:, _C_O:_C_CQ]
    def gates(pre, is_forget):
        logsig = jnp.minimum(pre, 0.0) - jnp.log1p(jnp.exp(-jnp.abs(pre)))
        return jnp.where(is_forget, logsig, pre)
    gc = z[:, _C_G:_C_G + 2 * M_HEADS] + bg_ref[...]
    g_ref[...] = gates(gc, lax.broadcasted_iota(jnp.int32, gc.shape, 1) >= M_HEADS)
    gr = _dot_nt(wgt_ref[...], xb)[:2 * M_HEADS] + bgt_ref[...]
    gt_ref[...] = gates(gr, lax.broadcasted_iota(jnp.int32, gr.shape, 0) >= M_HEADS)
    kt_ref[...] = _dot_nt(wt_ref[...], xb).astype(BF16)
    cq = _rms_norm(z[:, _C_CQ:_C_CKV], qg_ref[...]).astype(BF16)
    qat = _dot_nt(wqt_ref[...], cq)
    ckv = _rms_norm(z[:, _C_CKV:_C_G], kvg_ref[...])
    ckv_ref[...] = ckv
    ckv_b = ckv.astype(BF16)
    kn = _dot(ckv_b, wuk_ref[...])
    vt = _dot_nt(wuvt_ref[...], ckv_b)
    kr = z[:, _C_KRA:_C_KRB] * cos_ref[...] + z[:, _C_KRB:_C_END] * sin_ref[...]
    kr_ref[...] = kr[:, :A_ROPE]
    cost = cost_ref[...]
    sint = sint_ref[...]
    ones_row = (lax.broadcasted_iota(jnp.int32, (V_ROWS, 1), 0) == A_DV).astype(F32)
    nh = A_HEADS * HEAD_TILE
    for h in range(A_HEADS):
        sl = slice(h * HEAD_TILE, (h + 1) * HEAD_TILE)
        sw = slice(nh + h * HEAD_TILE, nh + (h + 1) * HEAD_TILE)
        aqt_ref[h] = ((qat[sl] * cost + qat[sw] * sint) * (ATTN_SCALE * LOG2E)).astype(BF16)
        ak_ref[h] = (kn[:, sl] + kr).astype(BF16)
        avt_ref[h] = (vt[h * V_ROWS:(h + 1) * V_ROWS] + ones_row).astype(BF16)


def _mixer_inputs(x, wts, tables, tm, n_tab_blocks):
    n = x.shape[0]
    cos, sin = tables
    row = lambda w: pl.BlockSpec((tm, w), lambda i: (i, 0))
    colb = lambda r: pl.BlockSpec((r, tm), lambda i: (0, i))
    head = pl.BlockSpec((A_HEADS, tm, HEAD_TILE), lambda i: (0, i, 0))
    head_t = lambda r: pl.BlockSpec((A_HEADS, r, tm), lambda i: (0, 0, i))
    tab = pl.BlockSpec((tm, HEAD_TILE), lambda i: (i % n_tab_blocks, 0))
    tab_t = pl.BlockSpec((HEAD_TILE, tm), lambda i: (0, i % n_tab_blocks))
    consts = [wts["w_main"], wts["w_kt"], wts["w_gt"], wts["bg_row"], wts["bg_col"], wts["q_norm_g"],
              wts["w_q2t"], wts["kv_norm_g"], wts["w_uk_tiles"], wts["w_uvt"]]
    hq = M_HEADS * HEAD_TILE
    hv = M_HEADS * M_DV
    out_shape = (
        jax.ShapeDtypeStruct((n, hq), BF16),
        jax.ShapeDtypeStruct((hq, n), BF16),
        jax.ShapeDtypeStruct((n, hv), BF16),
        jax.ShapeDtypeStruct((n, hv), F32),
        jax.ShapeDtypeStruct((n, 2 * M_HEADS), F32),
        jax.ShapeDtypeStruct((2 * M_HEADS, n), F32),
        jax.ShapeDtypeStruct((A_HEADS, HEAD_TILE, n), BF16),
        jax.ShapeDtypeStruct((A_HEADS, n, HEAD_TILE), BF16),
        jax.ShapeDtypeStruct((A_HEADS, V_ROWS, n), BF16),
        jax.ShapeDtypeStruct((n, KV_LORA), F32),
        jax.ShapeDtypeStruct((n, A_ROPE), F32),
    )
    out_specs = (row(hq), colb(hq), row(hv), row(hv), row(2 * M_HEADS), colb(2 * M_HEADS),
                 head_t(HEAD_TILE), head, head_t(V_ROWS), row(KV_LORA), row(A_ROPE))
    return pl.pallas_call(
        _mixin_kernel,
        grid=(n // tm,),
        in_specs=[row(D_MODEL)] + [_const_spec(c.shape) for c in consts] + [tab, tab, tab_t, tab_t],
        out_specs=out_specs,
        out_shape=out_shape,
        compiler_params=_cparams(("parallel",)),
        name="mixer_inputs",
    )(x, *consts, cos, sin, cos.T, sin.T)


def _split3(x):
    hi = x.astype(BF16)
    r1 = x - hi.astype(F32)
    mid = r1.astype(BF16)
    lo = (r1 - mid.astype(F32)).astype(BF16)
    return hi, mid, lo


def _mlstm_kernel(q_ref, kt_ref, v_ref, g_ref, gt_ref, c0_ref, m0_ref, h_ref, c_ref, m_ref, c_sc, m_sc,
                  *, chunk):
    ci = pl.program_id(1)

    @pl.when(ci == 0)
    def _():
        c_sc[...] = c0_ref[0]
        m_sc[...] = m0_ref[0]

    L = chunk
    row = lax.broadcasted_iota(jnp.int32, (L, L), 0)
    col = lax.broadcasted_iota(jnp.int32, (L, L), 1)
    causal = row >= col
    tril = causal.astype(BF16)
    triu = (row <= col).astype(BF16)
    g = g_ref[...]
    gt = gt_ref[...]
    cum_c = sum(_dot(tril, p) for p in _split3(g))
    cum_r = sum(_dot(p, triu) for p in _split3(gt))
    ones_col = (lax.broadcasted_iota(jnp.int32, (L, LANES), 1) == 0).astype(BF16)
    for h in range(M_HEADS):
        sl = slice(h * HEAD_TILE, (h + 1) * HEAD_TILE)
        qh = q_ref[:, sl]
        kth = kt_ref[sl, :]
        v_aug = jnp.concatenate([v_ref[:, h * M_DV:(h + 1) * M_DV], ones_col], axis=1)
        ig_r = gt[h:h + 1, :]
        lf_r = gt[M_HEADS + h:M_HEADS + h + 1, :]
        cr = cum_r[M_HEADS + h:M_HEADS + h + 1, :]
        cc = cum_c[:, M_HEADS + h:M_HEADS + h + 1]
        m_run = m_sc[h, 0:1, 0:1]
        d = jnp.where(causal, cc - cr + ig_r, -jnp.inf)
        inter = cc + m_run
        m_t = jnp.maximum(inter, jnp.max(d, axis=1, keepdims=True))
        s = (_dot(qh, kth) * jnp.exp(d - m_t)).astype(BF16)
        w_inter = jnp.exp(inter - m_t)
        c_state = c_sc[h]
        r = _dot(s, v_aug) + w_inter * _dot(qh, c_state.astype(BF16))
        qn = r[:, M_DV:M_DV + 1]
        den = jnp.maximum(jnp.abs(qn), jnp.exp(-m_t))
        h_ref[:, h * M_DV:(h + 1) * M_DV] = r[:, :M_DV] / den
        last = jnp.sum(lf_r, axis=1, keepdims=True)
        g_r = last - cr + ig_r
        m_new = jnp.maximum(last + m_run, jnp.max(g_r, axis=1, keepdims=True))
        decay = jnp.exp(last + m_run - m_new)
        wkt = (kth.astype(F32) * jnp.exp(g_r - m_new)).astype(BF16)
        c_sc[h] = decay * c_state + _dot(wkt, v_aug)
        m_sc[h] = jnp.broadcast_to(m_new, (8, LANES))

    @pl.when(ci == pl.num_programs(1) - 1)
    def _():
        c_ref[0] = c_sc[...]
        m_ref[0] = m_sc[...]


def _mlstm_scan(q, kt, v, g, gt, c0, m0, batch, seq, chunk):
    nc = seq // chunk
    hq = M_HEADS * HEAD_TILE
    hv = M_HEADS * M_DV
    row = lambda w: pl.BlockSpec((chunk, w), lambda b, c: (b * nc + c, 0))
    colb = lambda r: pl.BlockSpec((r, chunk), lambda b, c: (0, b * nc + c))
    st_in = lambda s: pl.BlockSpec((1,) + s, lambda b, c: (0,) * (len(s) + 1))
    st_out = lambda s: pl.BlockSpec((1,) + s, lambda b, c: (b,) + (0,) * len(s))
    cs, ms = (M_HEADS, HEAD_TILE, 2 * LANES), (M_HEADS, 8, LANES)
    return pl.pallas_call(
        functools.partial(_mlstm_kernel, chunk=chunk),
        grid=(batch, nc),
        in_specs=[row(hq), colb(hq), row(hv), row(2 * M_HEADS), colb(2 * M_HEADS), st_in(cs), st_in(ms)],
        out_specs=(row(hv), st_out(cs), st_out(ms)),
        out_shape=(jax.ShapeDtypeStruct((batch * seq, hv), F32),
                   jax.ShapeDtypeStruct((batch,) + cs, F32),
                   jax.ShapeDtypeStruct((batch,) + ms, F32)),
        scratch_shapes=[pltpu.VMEM(cs, F32), pltpu.VMEM(ms, F32)],
        compiler_params=_cparams(("parallel", "arbitrary")),
        name="mlstm_scan",
    )(q, kt, v, g, gt, c0, m0)


def _mlstm_step_kernel(q_ref, k_ref, kt_ref, v_ref, gc_ref, gr_ref, c0_ref, n0_ref, m0_ref,
                       h_ref, c_ref, n_ref, m_ref):
    T = q_ref.shape[1]
    row = lax.broadcasted_iota(jnp.int32, (T, T), 0)
    col = lax.broadcasted_iota(jnp.int32, (T, T), 1)
    causal = (row >= col)[None]
    q = q_ref[...]
    k = k_ref[...]
    v = v_ref[...]
    ig_c, lf_c = gc_ref[:, :, 0:1], gc_ref[:, :, 1:2]
    ig_r, lf_r = gr_ref[:, 0:1, :], gr_ref[:, 1:2, :]
    m0 = m0_ref[...]
    c0 = c0_ref[...]
    n0 = n0_ref[...]
    cum_c = jnp.sum(jnp.where(causal, lf_r, 0.0), axis=2, keepdims=True)
    cum_r = jnp.sum(jnp.where((row <= col)[None], lf_c, 0.0), axis=1, keepdims=True)
    d = jnp.where(causal, cum_c - cum_r + ig_r, -jnp.inf)
    inter = cum_c + m0
    m_t = jnp.maximum(inter, jnp.max(d, axis=2, keepdims=True))
    qk = jnp.einsum("gtd,gsd->gts", q, k, preferred_element_type=F32)
    s = qk * jnp.exp(d - m_t)
    w_inter = jnp.exp(inter - m_t)
    num = (jnp.einsum("gts,gsv->gtv", s.astype(BF16), v, preferred_element_type=F32)
           + w_inter * jnp.einsum("gtd,gdv->gtv", q, c0.astype(BF16), preferred_element_type=F32))
    qn = jnp.sum(s, axis=2, keepdims=True) + w_inter * jnp.sum(q.astype(F32) * n0, axis=2, keepdims=True)
    h_ref[...] = num / jnp.maximum(jnp.abs(qn), jnp.exp(-m_t))
    last = cum_c[:, T - 1:T, :]
    g_c = last - cum_c + ig_c
    g_r = last - cum_r + ig_r
    m_new = jnp.maximum(last + m0, jnp.max(g_c, axis=1, keepdims=True))
    decay = jnp.exp(last + m0 - m_new)
    wk = k.astype(F32) * jnp.exp(g_c - m_new)
    wkt = (kt_ref[...].astype(F32) * jnp.exp(g_r - m_new)).astype(BF16)
    c_ref[...] = decay * c0 + jnp.einsum("gds,gsv->gdv", wkt, v, preferred_element_type=F32)
    n_ref[...] = decay * n0 + jnp.sum(wk, axis=1, keepdims=True)
    m_ref[...] = m_new


def _mlstm_step(q, k, kt, v, gc, gr, c0, n0, m0, gb):
    gtot, T = q.shape[0], q.shape[1]
    blk = lambda a, b: pl.BlockSpec((gb, a, b), lambda i: (i, 0, 0))
    return pl.pallas_call(
        _mlstm_step_kernel,
        grid=(gtot // gb,),
        in_specs=[blk(T, M_DQK), blk(T, M_DQK), blk(M_DQK, T), blk(T, M_DV), blk(T, 2), blk(2, T),
                  blk(M_DQK, M_DV), blk(1, M_DQK), blk(1, 1)],
        out_specs=(blk(T, M_DV), blk(M_DQK, M_DV), blk(1, M_DQK), blk(1, 1)),
        out_shape=(jax.ShapeDtypeStruct((gtot, T, M_DV), F32),
                   jax.ShapeDtypeStruct((gtot, M_DQK, M_DV), F32),
                   jax.ShapeDtypeStruct((gtot, 1, M_DQK), F32),
                   jax.ShapeDtypeStruct((gtot, 1, 1), F32)),
        compiler_params=_cparams(("parallel",)),
        name="mlstm_step",
    )(q, k, kt, v, gc, gr, c0, n0, m0)


def _attn_kernel(qi_tab, ki_tab, qt_ref, k_ref, vt_ref, km_ref, vmt_ref, o_ref, acc_sc, m_sc):
    step = pl.program_id(1)
    qi = qi_tab[step]
    ki = ki_tab[step]
    tq = qt_ref.shape[2]

    @pl.when(ki == 0)
    def _():
        ss = [_dot(km_ref[h], qt_ref[h]) for h in range(A_HEADS)]
        ms = [jnp.max(s, axis=0, keepdims=True) for s in ss]
        for h in range(A_HEADS):
            acc_sc[h] = _dot(vmt_ref[h], jnp.exp2(ss[h] - ms[h]).astype(BF16))
            m_sc[h] = ms[h]

    @pl.when(ki < qi)
    def _():
        s_next = _dot(k_ref[0], qt_ref[0])
        for h in range(A_HEADS):
            s = s_next
            if h + 1 < A_HEADS:
                s_next = _dot(k_ref[h + 1], qt_ref[h + 1])
            m_prev = m_sc[h]
            m_new = jnp.maximum(m_prev, jnp.max(s, axis=0, keepdims=True))
            p = jnp.exp2(s - m_new).astype(BF16)
            acc_sc[h] = jnp.exp2(m_prev - m_new) * acc_sc[h] + _dot(vt_ref[h], p)
            m_sc[h] = m_new

    def update_diagonal():
        hk = tq // 2
        row = lax.broadcasted_iota(jnp.int32, (hk, tq), 0)
        col = lax.broadcasted_iota(jnp.int32, (hk, tq), 1)
        keep_a = row <= col
        keep_b = keep_a[:, :hk]
        scores_a = lambda h: _dot(k_ref[h, :hk], qt_ref[h])
        s_next = scores_a(0)
        for h in range(A_HEADS):
            s1 = jnp.where(keep_a, s_next, -jnp.inf)
            s2 = jnp.where(keep_b, _dot(k_ref[h, hk:], qt_ref[h, :, hk:]), -jnp.inf)
            if h + 1 < A_HEADS:
                s_next = scores_a(h + 1)
            m_prev = m_sc[h]
            m1 = jnp.maximum(m_prev, jnp.max(s1, axis=0, keepdims=True))
            m2 = jnp.maximum(m1[:, hk:], jnp.max(s2, axis=0, keepdims=True))
            m_new = jnp.concatenate([m1[:, :hk], m2], axis=1)
            p1 = jnp.exp2(s1 - m_new).astype(BF16)
            p2 = jnp.exp2(s2 - m2).astype(BF16)
            pv = _dot(vt_ref[h, :, :hk], p1)
            pv = pv + jnp.concatenate([jnp.zeros((V_ROWS, hk), F32), _dot(vt_ref[h, :, hk:], p2)], axis=1)
            acc_sc[h] = jnp.exp2(m_prev - m_new) * acc_sc[h] + pv
            m_sc[h] = m_new

    @pl.when(ki == qi)
    def _():
        update_diagonal()
        for j in range(A_HEADS // 2):
            ae = acc_sc[2 * j]
            ao = acc_sc[2 * j + 1]
            pair = jnp.concatenate([ae[:A_DV] / ae[A_DV:A_DV + 1], ao[:A_DV] / ao[A_DV:A_DV + 1]], axis=0)
            o_ref[:, j * HEAD_TILE:(j + 1) * HEAD_TILE] = pair.T.astype(o_ref.dtype)


def _attention_prompt(aqt, ak, avt, k_meta, vt_meta, batch, seq):
    blk = ATTN_BLOCK
    nq = seq // blk
    qi_np, ki_np = zip(*[(i, j) for i in range(nq) for j in range(i + 1)])
    qi_tab = jnp.asarray(np.array(qi_np, np.int32))
    ki_tab = jnp.asarray(np.array(ki_np, np.int32))
    rows = lambda tab: pl.BlockSpec((A_HEADS, blk, HEAD_TILE),
                                    lambda b, s, qt, kt: (0, b * nq + (qt, kt)[tab][s], 0))
    cols = lambda tab, r: pl.BlockSpec((A_HEADS, r, blk),
                                       lambda b, s, qt, kt: (0, 0, b * nq + (qt, kt)[tab][s]))
    grid_spec = pltpu.PrefetchScalarGridSpec(
        num_scalar_prefetch=2,
        grid=(batch, len(qi_np)),
        in_specs=[cols(0, HEAD_TILE), rows(1), cols(1, V_ROWS),
                  pl.BlockSpec((A_HEADS, N_META, HEAD_TILE), lambda b, s, qt, kt: (0, 0, 0)),
                  pl.BlockSpec((A_HEADS, V_ROWS, N_META), lambda b, s, qt, kt: (0, 0, 0))],
        out_specs=pl.BlockSpec((blk, A_HEADS * A_DV), lambda b, s, qt, kt: (b * nq + qt[s], 0)),
        scratch_shapes=[pltpu.VMEM((A_HEADS, V_ROWS, blk), F32), pltpu.VMEM((A_HEADS, 1, blk), F32)],
    )
    return pl.pallas_call(
        _attn_kernel,
        grid_spec=grid_spec,
        out_shape=jax.ShapeDtypeStruct((batch * seq, A_HEADS * A_DV), BF16),
        compiler_params=_cparams(("parallel", "arbitrary")),
        name="attn_prompt",
    )(qi_tab, ki_tab, aqt, ak, avt, k_meta, vt_meta)


def _absorb_kernel(qt_ref, wuk_ref, o_ref):
    for h in range(A_HEADS):
        o_ref[h] = _dot(wuk_ref[h], qt_ref[h]).astype(BF16)


def _absorb_q(aqt, wuk_tiles):
    n = aqt.shape[2]
    return pl.pallas_call(
        _absorb_kernel,
        out_shape=jax.ShapeDtypeStruct((A_HEADS, KV_LORA, n), BF16),
        compiler_params=pltpu.CompilerParams(vmem_limit_bytes=VMEM_LIMIT),
        name="absorb_q",
    )(aqt, wuk_tiles)


def _decode_attn_kernel(pt_ref, ql_ref, qr_ref, cn_ref, kn_ref, lat_hbm, krt_hbm, o_ref,
                        lat_buf, krt_buf, sem_lat, sem_krt):
    g = pl.program_id(0)
    ng = pl.num_programs(0)
    sps, sub, chunk, n_slots = DEC_SEQS, DEC_SUB, DEC_CHUNK, DEC_SLOTS
    n_sub_seq = pt_ref.shape[1] // sub
    spc = chunk // sub
    n_sub = sps * n_sub_seq
    ahead = 2 * spc
    assert n_sub_seq % spc == 0 and (n_sub // spc) % n_slots == 0 and n_slots >= 4

    def copies(step, j):
        seq = step * sps + j // n_sub_seq
        slot = (j // spc) % n_slots
        for k in range(sub):
            page = pt_ref[seq, (j % n_sub_seq) * sub + k]
            pos = (j % spc) * sub + k
            yield pltpu.make_async_copy(lat_hbm.at[page], lat_buf.at[slot, pos], sem_lat.at[slot])
            yield pltpu.make_async_copy(krt_hbm.at[page], krt_buf.at[slot, pos], sem_krt.at[slot])

    def start(step, j):
        for n, cp in enumerate(copies(step, j)):
            cp.start(priority=n % 2)

    def wait_chunk(c):
        for j in range(c * spc, (c + 1) * spc):
            for cp in copies(g, j):
                cp.wait()

    @pl.when(g == 0)
    def _():
        for j in range(ahead):
            start(0, j)

    rows = ql_ref.shape[1]

    def scores(j):
        q = j // n_sub_seq
        slot = (j // spc) % n_slots
        ks = range((j % spc) * sub, (j % spc + 1) * sub)
        lats = [lat_buf[slot, k].astype(BF16) for k in ks]
        s = jnp.concatenate(
            [_dot_nt(ql_ref[q], lat) + _dot(qr_ref[q], krt_buf[slot, k].astype(BF16)) for k, lat in zip(ks, lats)],
            axis=1)
        return lats, s

    def finish(q, m_run, l_run, acc):
        T = cn_ref.shape[1]
        cn = cn_ref[q].astype(BF16).astype(F32)
        kn = kn_ref[q].astype(BF16).astype(F32)
        qlf = ql_ref[q].astype(F32)
        qrf = qr_ref[q].astype(F32)
        t_of_row = lax.broadcasted_iota(jnp.int32, (rows, 1), 0) % T
        s_new = []
        for t in range(T):
            st = (jnp.sum(qlf * cn[t:t + 1], axis=1, keepdims=True)
                  + jnp.sum(qrf * kn[t:t + 1], axis=1, keepdims=True))
            s_new.append(jnp.where(t_of_row >= t, st, -jnp.inf))
        m_fin = m_run
        for st in s_new:
            m_fin = jnp.maximum(m_fin, st)
        alpha = jnp.exp2(m_run - m_fin)
        acc = alpha * acc
        l_run = alpha * l_run
        for t, st in enumerate(s_new):
            pt = jnp.exp2(st - m_fin)
            l_run = l_run + pt
            acc = acc + pt * cn[t:t + 1]
        o_ref[q] = acc / l_run

    wait_chunk(0)
    nxt = scores(0)
    for j in range(n_sub):
        if j % n_sub_seq == 0:
            m_run = jnp.full((rows, 1), -jnp.inf, F32)
            l_run = jnp.zeros((rows, 1), F32)
            acc = jnp.zeros((rows, KV_LORA), F32)
        lats, s = nxt
        if j + 1 < n_sub:
            if (j + 1) % spc == 0:
                wait_chunk((j + 1) // spc)
            nxt = scores(j + 1)
        jj = j + ahead
        if jj < n_sub:
            start(g, jj)
        else:
            @pl.when(g + 1 < ng)
            def _():
                start(g + 1, jj - n_sub)
        m_new = jnp.maximum(m_run, jnp.max(s, axis=1, keepdims=True))
        p = jnp.exp2(s - m_new)
        alpha = jnp.exp2(m_run - m_new)
        l_run = alpha * l_run + jnp.sum(p, axis=1, keepdims=True)
        pb = p.astype(BF16)
        pv = sum(_dot(pb[:, k * PAGE_SIZE:(k + 1) * PAGE_SIZE], lat) for k, lat in enumerate(lats))
        acc = alpha * acc + pv
        m_run = m_new
        if j % n_sub_seq == n_sub_seq - 1:
            finish(j // n_sub_seq, m_run, l_run, acc)


def _decode_attention(page_table, q_lat, q_rope, c_new, kr_new, cache_lat, cache_krt):
    nb = page_table.shape[0]
    rows = q_lat.shape[1]
    per_b = lambda a, b: pl.BlockSpec((DEC_SEQS, a, b), lambda i, pt: (i, 0, 0))
    hbm = pl.BlockSpec(memory_space=pl.ANY)
    grid_spec = pltpu.PrefetchScalarGridSpec(
        num_scalar_prefetch=1,
        grid=(nb // DEC_SEQS,),
        in_specs=[per_b(rows, KV_LORA), per_b(rows, A_ROPE), per_b(c_new.shape[1], KV_LORA),
                  per_b(kr_new.shape[1], A_ROPE), hbm, hbm],
        out_specs=per_b(rows, KV_LORA),
        scratch_shapes=[pltpu.VMEM((DEC_SLOTS, DEC_CHUNK, PAGE_SIZE, KV_LORA), F32),
                        pltpu.VMEM((DEC_SLOTS, DEC_CHUNK, A_ROPE, PAGE_SIZE), F32),
                        pltpu.SemaphoreType.DMA((DEC_SLOTS,)), pltpu.SemaphoreType.DMA((DEC_SLOTS,))],
    )
    return pl.pallas_call(
        _decode_attn_kernel,
        grid_spec=grid_spec,
        out_shape=jax.ShapeDtypeStruct((nb, rows, KV_LORA), F32),
        compiler_params=_cparams(("arbitrary",)),
        name="attn_decode",
    )(page_table, q_lat, q_rope, c_new, kr_new, cache_lat, cache_krt)


def _lat2attn_kernel(o_ref, wuv_ref, a_ref):
    for j in range(A_HEADS // 2):
        pair = (_dot(o_ref[2 * j].astype(BF16), wuv_ref[2 * j])
                + _dot(o_ref[2 * j + 1].astype(BF16), wuv_ref[2 * j + 1]))
        a_ref[:, j * HEAD_TILE:(j + 1) * HEAD_TILE] = pair.astype(a_ref.dtype)


def _lat2attn(o_lat, wuv_tiles):
    n = o_lat.shape[1]
    return pl.pallas_call(
        _lat2attn_kernel,
        out_shape=jax.ShapeDtypeStruct((n, A_HEADS * A_DV), BF16),
        compiler_params=pltpu.CompilerParams(vmem_limit_bytes=VMEM_LIMIT),
        name="lat2attn",
    )(o_lat, wuv_tiles)


def _mixout_kernel(x_ref, hm_ref, om_ref, at_ref, ng_ref, wo_ref, g2_ref, b2_ref,
                   w1_ref, w3_ref, w2_ref, g3_ref, b3_ref, o_ref):
    x = x_ref[...]
    parts = []
    for h in range(M_HEADS):
        sl = slice(h * M_DV, (h + 1) * M_DV)
        hn = _rms_norm(hm_ref[:, sl], ng_ref[:, sl])
        parts.append((hn * jax.nn.sigmoid(om_ref[:, sl])).astype(BF16))
    cat = jnp.concatenate(parts + [at_ref[...]], axis=1)
    x2 = _layer_norm(ALPHA * x + _dot(cat, wo_ref[...]), g2_ref[...], b2_ref[...])
    o_ref[...] = _swiglu_ln(x2, w1_ref, w3_ref, w2_ref, g3_ref[...], b3_ref[...])


def _mixer_output_ffn(x, hm, om, attn, wts, tm):
    n = x.shape[0]
    row = lambda w: pl.BlockSpec((tm, w), lambda i: (i, 0))
    consts = [wts["mlstm_norm_g"], wts["w_out"], wts["ln2_g"], wts["ln2_b"],
              wts["ffn2_w1"], wts["ffn2_w3"], wts["ffn2_w2"], wts["ln3_g"], wts["ln3_b"]]
    hv = M_HEADS * M_DV
    return pl.pallas_call(
        _mixout_kernel,
        grid=(n // tm,),
        in_specs=[row(D_MODEL), row(hv), row(hv), row(A_HEADS * A_DV)] + [_const_spec(c.shape) for c in consts],
        out_specs=row(D_MODEL),
        out_shape=jax.ShapeDtypeStruct((n, D_MODEL), F32),
        compiler_params=_cparams(("parallel",)),
        name="mixer_output_ffn",
    )(x, hm, om, attn, *consts)


def _pack_weights(w_in, b_gates, mlstm_norm_g, q_norm_g, w_uq, kv_norm_g, w_uk, w_uv, w_out):
    o_q, o_k, o_v, o_o = 0, M_HEADS * M_DQK, 2 * M_HEADS * M_DQK, 2 * M_HEADS * M_DQK + M_HEADS * M_DV
    o_g = o_o + M_HEADS * M_DV
    o_cq = o_g + 2 * M_HEADS
    o_ckv = o_cq + Q_LORA
    o_kr = o_ckv + KV_LORA
    half = A_ROPE // 2
    dq = A_NOPE + A_ROPE

    def pad_axis(a, axis, lo, size):
        pads = [(0, 0)] * a.ndim
        pads[axis] = (lo, size - lo - a.shape[axis])
        return jnp.pad(a, pads)

    q_cols = pad_axis(w_in[:, o_q:o_k].reshape(D_MODEL, M_HEADS, M_DQK), 2, 0, HEAD_TILE).reshape(D_MODEL, -1)
    kr = w_in[:, o_kr:o_kr + A_ROPE]
    kr_sw = jnp.concatenate([kr[:, half:], kr[:, :half]], 1)
    w_main = jnp.concatenate(
        [q_cols, w_in[:, o_v:o_g], w_in[:, o_cq:o_kr], pad_axis(w_in[:, o_g:o_cq], 1, 0, LANES),
         pad_axis(kr, 1, 0, LANES), pad_axis(kr_sw, 1, 0, LANES)], 1)
    assert w_main.shape[1] == _C_END
    w_kt = pad_axis((w_in[:, o_k:o_v].T * (M_DQK ** -0.5)).reshape(M_HEADS, M_DQK, D_MODEL), 1, 0, HEAD_TILE)
    w_kt = w_kt.reshape(M_HEADS * HEAD_TILE, D_MODEL)
    w_gt = pad_axis(w_in[:, o_g:o_cq].T, 0, 0, 16)
    uq = w_uq.reshape(Q_LORA, A_HEADS, dq)
    q_tiles = pad_axis(jnp.concatenate([uq[..., A_NOPE:], uq[..., :A_NOPE]], -1), 2, 0, HEAD_TILE)
    q_sw_tiles = pad_axis(jnp.concatenate([uq[..., A_NOPE + half:], uq[..., A_NOPE:A_NOPE + half]], -1),
                          2, 0, HEAD_TILE)
    w_q2 = jnp.concatenate([q_tiles.reshape(Q_LORA, -1), q_sw_tiles.reshape(Q_LORA, -1)], 1)
    uk_tiles = pad_axis(w_uk, 2, A_ROPE, HEAD_TILE)
    w_uvt = pad_axis(w_uv.transpose(1, 2, 0), 1, 0, V_ROWS).reshape(A_HEADS * V_ROWS, KV_LORA)
    uv_pairs = w_uv.reshape(KV_LORA, A_HEADS // 2, 2, A_DV)
    uv_heads = jnp.stack([pad_axis(uv_pairs[:, :, 0], 2, 0, HEAD_TILE),
                          pad_axis(uv_pairs[:, :, 1], 2, A_DV, HEAD_TILE)], 2)
    bf = lambda a: a.astype(BF16)
    return {
        "w_main": bf(w_main), "w_kt": bf(w_kt), "w_gt": bf(w_gt),
        "bg_row": b_gates.reshape(1, -1), "bg_col": b_gates.reshape(-1, 1),
        "q_norm_g": q_norm_g.reshape(1, -1), "kv_norm_g": kv_norm_g.reshape(1, -1),
        "w_q2t": bf(w_q2.T),
        "w_uk_tiles": bf(uk_tiles.reshape(KV_LORA, -1)),
        "w_uvt": bf(w_uvt),
        "w_uk_heads": bf(uk_tiles.transpose(1, 0, 2)),
        "w_uv_heads": bf(uv_heads.reshape(KV_LORA, A_HEADS, HEAD_TILE).transpose(1, 0, 2)),
        "mlstm_norm_g": mlstm_norm_g.reshape(1, -1), "w_out": bf(w_out),
    }


def _rope_tables(pos):
    half = A_ROPE // 2
    inv = ROPE_BASE ** (-np.arange(half, dtype=np.float64) / half)
    ang = np.asarray(pos, np.float64)[:, None] * inv[None, :]
    cos, sin = np.cos(ang), np.sin(ang)
    n = ang.shape[0]
    cos_t = np.concatenate([cos, cos, np.ones((n, A_NOPE)), np.zeros((n, HEAD_TILE - A_NOPE - A_ROPE))], 1)
    sin_t = np.concatenate([-sin, sin, np.zeros((n, HEAD_TILE - A_ROPE))], 1)
    return cos_t.astype(np.float32), sin_t.astype(np.float32)


def kernel(x_prompt, x_sample, cache_kv_latent, cache_k_rope, state_mlstm_C, state_mlstm_n, state_mlstm_m, page_table, meta_tokens, ffn1_w1, ffn1_w3, ffn1_w2, ln1_g, ln1_b, w_in, b_gates, mlstm_norm_g, q_norm_g, w_uq, kv_norm_g, w_uk, w_uv, w_out, ln2_g, ln2_b, ffn2_w1, ffn2_w3, ffn2_w2, ln3_g, ln3_b):
    assert w_in.shape[0] == DEPTH == 1
    B, S, _ = x_prompt.shape
    Bd, Td, _ = x_sample.shape
    n_real, n_dec = B * S, Bd * Td
    l = 0
    vec = lambda a: a[l].reshape(1, -1)
    wts = _pack_weights(w_in[l], b_gates[l], mlstm_norm_g[l], q_norm_g[l], w_uq[l], kv_norm_g[l],
                        w_uk[l], w_uv[l], w_out[l])
    ffn_bf = _to_bf16([ffn1_w1[l], ffn1_w3[l], ffn1_w2[l], ffn2_w1[l], ffn2_w3[l], ffn2_w2[l]])
    wts.update({"ln2_g": vec(ln2_g), "ln2_b": vec(ln2_b), "ln3_g": vec(ln3_g), "ln3_b": vec(ln3_b),
                "ffn2_w1": ffn_bf[3], "ffn2_w3": ffn_bf[4], "ffn2_w2": ffn_bf[5]})
    f1 = (ffn_bf[0], ffn_bf[1], ffn_bf[2], vec(ln1_g), vec(ln1_b))

    x_real = x_prompt.reshape(n_real, D_MODEL)
    x_extra = jnp.concatenate([meta_tokens.astype(F32), x_sample.reshape(n_dec, D_MODEL)], 0)
    n_extra = x_extra.shape[0]
    tm = 512

    h_real = _ffn_ln(x_real, *f1, tm)
    h_extra = _ffn_ln(x_extra, *f1, n_extra)

    pos_e = np.concatenate([np.arange(N_META), np.tile(page_table.shape[1] * PAGE_SIZE + np.arange(Td), Bd)])
    (q_r, kt_r, v_r, om_r, g_r, gt_r, aqt_r, ak_r, avt_r, ckv_r, kr_r) = _mixer_inputs(
        h_real, wts, _rope_tables(N_META + np.arange(S)), tm, S // tm)
    (q_e, kt_e, v_e, om_e, g_e, gt_e, aqt_e, ak_e, avt_e, ckv_e, kr_e) = _mixer_inputs(
        h_extra, wts, _rope_tables(pos_e), n_extra, 1)

    cs, ms = (1, M_HEADS, HEAD_TILE, 2 * LANES), (1, M_HEADS, 8, LANES)
    _, c_meta, m_meta = _mlstm_scan(q_e[:N_META], kt_e[:, :N_META], v_e[:N_META], g_e[:N_META],
                                    gt_e[:, :N_META], jnp.zeros(cs, F32), jnp.zeros(ms, F32), 1, N_META, N_META)
    hm_r, c_p, m_p = _mlstm_scan(q_r, kt_r, v_r, g_r, gt_r, c_meta, m_meta, B, S, MLSTM_CHUNK)
    mlstm_c_p = c_p[:, :, :M_DQK, :M_DV][None]
    mlstm_n_p = c_p[:, :, :M_DQK, M_DV][None]
    mlstm_m_p = m_p[:, :, 0, 0][None]

    G = Bd * M_HEADS
    def dec_heads(a, w, used):
        return a.reshape(Bd, Td, M_HEADS, w)[..., :used].transpose(0, 2, 1, 3).reshape(G, Td, used)
    q_d = dec_heads(q_e[N_META:], HEAD_TILE, M_DQK)
    v_d = dec_heads(v_e[N_META:], M_DV, M_DV)
    kt_d = (kt_e[:, N_META:].reshape(M_HEADS, HEAD_TILE, Bd, Td)[:, :M_DQK]
            .transpose(2, 0, 1, 3).reshape(G, M_DQK, Td))
    k_d = kt_d.transpose(0, 2, 1)
    gdec = g_e[N_META:].reshape(Bd, Td, 2, M_HEADS).transpose(0, 3, 1, 2).reshape(G, Td, 2)
    hm_d, c_d, n_d, m_d = _mlstm_step(
        q_d, k_d, kt_d, v_d, gdec, gdec.transpose(0, 2, 1),
        state_mlstm_C[l].reshape(G, M_DQK, M_DV), state_mlstm_n[l].reshape(G, 1, M_DQK),
        state_mlstm_m[l].reshape(G, 1, 1), 64)
    hm_d = hm_d.reshape(Bd, M_HEADS, Td, M_DV).transpose(0, 2, 1, 3).reshape(n_dec, M_HEADS * M_DV)

    attn_r = _attention_prompt(aqt_r, ak_r, avt_r, ak_e[:, :N_META], avt_e[:, :, :N_META], B, S)

    aqt_d = aqt_e[:, :, N_META:]
    q_lat_t = _absorb_q(aqt_d, wts["w_uk_heads"])
    rows = A_HEADS * Td
    by_seq = lambda a: (a.reshape(A_HEADS, a.shape[1], Bd, Td).transpose(2, 0, 3, 1)
                        .reshape(Bd, rows, a.shape[1]))
    o_lat = _decode_attention(
        page_table, by_seq(q_lat_t), by_seq(aqt_d[:, :A_ROPE]),
        ckv_e[N_META:].reshape(Bd, Td, KV_LORA), kr_e[N_META:].reshape(Bd, Td, A_ROPE),
        cache_kv_latent[l], jnp.swapaxes(cache_k_rope[l], 1, 2))
    o_lat = o_lat.reshape(Bd, A_HEADS, Td, KV_LORA).transpose(1, 0, 2, 3).reshape(A_HEADS, n_dec, KV_LORA)
    attn_d = _lat2attn(o_lat, wts["w_uv_heads"])

    y_real = _mixer_output_ffn(h_real, hm_r, om_r, attn_r, wts, tm)
    y_dec = _mixer_output_ffn(h_extra[N_META:], hm_d, om_e[N_META:], attn_d, wts, n_dec)

    bc = lambda a: jnp.broadcast_to(a[None], (B,) + a.shape)
    kv_p = jnp.concatenate([bc(ckv_e[:N_META]), ckv_r.reshape(B, S, KV_LORA)], 1)[None]
    kr_p = jnp.concatenate([bc(kr_e[:N_META]), kr_r.reshape(B, S, A_ROPE)], 1)[None]
    return (y_real.reshape(B, S, D_MODEL), y_dec.reshape(Bd, Td, D_MODEL), kv_p, kr_p,
            mlstm_c_p, mlstm_n_p, mlstm_m_p,
            ckv_e[N_META:].reshape(1, Bd, Td, KV_LORA), kr_e[N_META:].reshape(1, Bd, Td, A_ROPE),
            c_d.reshape(1, Bd, M_HEADS, M_DQK, M_DV), n_d.reshape(1, Bd, M_HEADS, M_DQK),
            m_d.reshape(1, Bd, M_HEADS))
```
